```python
import jax, jax.numpy as jnp
from jax import lax
import numpy as np

D_MODEL = 1024
BATCH = 1
SEQ = 16384
DEPTH = 1

CHUNK = 64
D_MIX = D_MODEL
CONV_WIDTH = D_MIX // 2
CONV_GROUPS = 8
CONV_K = 3
SGU_WIDTH = D_MIX - CONV_WIDTH
SGU_GROUPS = 4
SGU_BLOCK = 128
D_IN = 3 * CONV_WIDTH + 2 * SGU_WIDTH
PEER_HEADS = 8
N_KEYS = 128
N_EXPERTS = N_KEYS * N_KEYS
PEER_TOPK = 16
D_QUERY = 256
D_HALF = D_QUERY // 2
PEER_BLOCK = 128
EPS = 1e-6

kernel_name = "hybrid_shortconv_sgu_peer_block"


def rmsnorm(x, g):
    xf = x.astype(jnp.float32)
    xf = xf * lax.rsqrt(jnp.mean(xf * xf, axis=-1, keepdims=True) + EPS)
    return (xf * g.astype(jnp.float32)).astype(x.dtype)


def group_rmsnorm(y, g, groups):
    shp = y.shape
    yf = y.astype(jnp.float32).reshape(shp[:-1] + (groups, shp[-1] // groups))
    yf = yf * lax.rsqrt(jnp.mean(yf * yf, axis=-1, keepdims=True) + EPS)
    return (yf.reshape(shp) * g.astype(jnp.float32)).astype(y.dtype)


def short_conv_mixer(b_gate, c_gate, h, conv_w):
    z = c_gate * h
    seq = z.shape[1]
    zp = jnp.pad(z, ((0, 0), (CONV_K - 1, 0), (0, 0)))
    conv = sum(conv_w[k] * zp[:, k:k + seq] for k in range(CONV_K))
    return b_gate * conv


def spatial_gating_mixer(u, v, w_s, b_s, g_v):
    bsz, seq, _ = u.shape
    nb = seq // SGU_BLOCK
    v = group_rmsnorm(v, g_v, SGU_GROUPS)
    vb = v.reshape(bsz, nb, SGU_BLOCK, SGU_GROUPS, SGU_WIDTH // SGU_GROUPS)
    chunk_id = jnp.arange(SGU_BLOCK) // CHUNK
    mask = chunk_id[:, None] >= chunk_id[None, :]
    w = jnp.where(mask[None], w_s, jnp.zeros((), w_s.dtype))
    mixed = jnp.einsum('gij,bnjgc->bnigc', w, vb) + b_s.T[None, None, :, :, None]
    return u * mixed.reshape(bsz, seq, SGU_WIDTH)


def peer_ffn(x, w_q, sub_keys, expert_u, expert_v):
    bsz, seq, d = x.shape
    xt = x.reshape(-1, PEER_BLOCK, d)

    def block(xb):
        t = xb.shape[0]
        q = (xb @ w_q).reshape(t, PEER_HEADS, 2, D_HALF)
        s = jnp.einsum('thpd,hpkd->thpk', q, sub_keys).astype(jnp.float32)
        v1, i1 = lax.top_k(s[:, :, 0], PEER_TOPK)
        v2, i2 = lax.top_k(s[:, :, 1], PEER_TOPK)
        cand = (v1[..., :, None] + v2[..., None, :]).reshape(t, PEER_HEADS, PEER_TOPK * PEER_TOPK)
        cidx = (i1[..., :, None] * N_KEYS + i2[..., None, :]).reshape(t, PEER_HEADS, PEER_TOPK * PEER_TOPK)
        top, pos = lax.top_k(cand, PEER_TOPK)
        idx = jnp.take_along_axis(cidx, pos, axis=-1)
        gates = jax.nn.softmax(top, axis=-1)
        u_sel = expert_u[idx]
        act = jax.nn.gelu(jnp.einsum('thkd,td->thk', u_sel, xb).astype(jnp.float32), approximate=False)
        v_sel = expert_v[idx]
        return jnp.einsum('thk,thkd->td', (gates * act).astype(xb.dtype), v_sel)

    y = lax.map(block, xt)
    return y.reshape(bsz, seq, d)


def setup_inputs(seed: int = 0) -> dict:
    key = jax.random.key(seed)
    ks = jax.random.split(key, 16)
    f32 = jnp.float32
    L = DEPTH
    nrm = lambda k, shp, s: (jax.random.normal(k, shp, f32) * s).astype(f32)
    return {
        "x": nrm(ks[0], (BATCH, SEQ, D_MODEL), 1.0),
        "attn_norm_g": 1.0 + nrm(ks[1], (L, D_MODEL), 0.02),
        "w_in": nrm(ks[2], (L, D_MODEL, D_IN), D_MODEL ** -0.5),
        "conv_w": nrm(ks[3], (L, CONV_K, CONV_WIDTH), CONV_K ** -0.5),
        "sgu_w": nrm(ks[4], (L, SGU_GROUPS, SGU_BLOCK, SGU_BLOCK), SGU_BLOCK ** -0.5),
        "sgu_b": 1.0 + nrm(ks[5], (L, SGU_GROUPS, SGU_BLOCK), 0.02),
        "sgu_norm_g": 1.0 + nrm(ks[6], (L, SGU_WIDTH), 0.02),
        "out_norm_conv_g": 1.0 + nrm(ks[7], (L, CONV_WIDTH), 0.02),
        "out_norm_sgu_g": 1.0 + nrm(ks[8], (L, SGU_WIDTH), 0.02),
        "w_out": nrm(ks[9], (L, D_MIX, D_MODEL), D_MIX ** -0.5),
        "ffn_norm_g": 1.0 + nrm(ks[10], (L, D_MODEL), 0.02),
        "peer_w_q": nrm(ks[11], (L, D_MODEL, PEER_HEADS * D_QUERY), D_MODEL ** -0.5),
        "peer_sub_keys": nrm(ks[12], (L, PEER_HEADS, 2, N_KEYS, D_HALF), D_HALF ** -0.5),
        "peer_u": nrm(ks[13], (L, N_EXPERTS, D_MODEL), D_MODEL ** -0.5),
        "peer_v": nrm(ks[14], (L, N_EXPERTS, D_MODEL), PEER_HEADS ** -0.5),
        "final_norm_g": 1.0 + nrm(ks[15], (D_MODEL,), 0.02),
    }


def reference(x, attn_norm_g, w_in, conv_w, sgu_w, sgu_b, sgu_norm_g, out_norm_conv_g,
              out_norm_sgu_g, w_out, ffn_norm_g, peer_w_q, peer_sub_keys, peer_u, peer_v,
              final_norm_g):
    h = x
    for l in range(DEPTH):
        xn = rmsnorm(h, attn_norm_g[l])
        proj = xn @ w_in[l]
        c0, c1, c2, c3 = CONV_WIDTH, 2 * CONV_WIDTH, 3 * CONV_WIDTH, 3 * CONV_WIDTH + SGU_WIDTH
        b_gate, c_gate, hc = proj[..., :c0], proj[..., c0:c1], proj[..., c1:c2]
        z = jax.nn.gelu(proj[..., c2:], approximate=False)
        u, v = z[..., :SGU_WIDTH], z[..., SGU_WIDTH:]
        y_conv = short_conv_mixer(b_gate, c_gate, hc, conv_w[l])
        y_sgu = spatial_gating_mixer(u, v, sgu_w[l], sgu_b[l], sgu_norm_g[l])
        y = jnp.concatenate([group_rmsnorm(y_conv, out_norm_conv_g[l], CONV_GROUPS),
                             group_rmsnorm(y_sgu, out_norm_sgu_g[l], SGU_GROUPS)], axis=-1)
        h = h + y @ w_out[l]
        hn = rmsnorm(h, ffn_norm_g[l])
        h = h + peer_ffn(hn, peer_w_q[l], peer_sub_keys[l], peer_u[l], peer_v[l])
    return rmsnorm(h, final_norm_g)
```

```python
import functools
import math

import jax
import jax.numpy as jnp
from jax import lax
from jax.experimental import pallas as pl
from jax.experimental.pallas import tpu as pltpu

F32 = jnp.float32
BF16 = jnp.bfloat16

EPS = 1e-6
CHUNK = 64
CONV_K = 3
CONV_GROUPS = 8
SGU_GROUPS = 4
SGU_BLOCK = 128
PEER_HEADS = 8
N_KEYS = 128
PEER_TOPK = 16
SQRT_HALF = math.sqrt(0.5)
NEG_BIG = -3.0e38
POS_BIG = 3.0e38

LANES = 128
SUBLANES = 8

MIXER_ROWS = 512
PEER_TOKENS = 512
PEER_KEY_ROWS = 8
MIXER_VMEM_BYTES = 48 * 1024 * 1024
PEER_VMEM_BYTES = 56 * 1024 * 1024

_CAND_ROWS = 56
_CAND_BLOCKS = []
_CAND_SINGLES = []
_off = 0
for _b in range(PEER_TOPK):
    _n = PEER_TOPK // (_b + 1)
    if _n >= 4:
        _off = -(-_off // SUBLANES) * SUBLANES
        _CAND_BLOCKS.append((_b, _n, _off))
        _off += _n
    else:
        for _a in range(_n):
            _CAND_SINGLES.append((_a, _b, _off))
            _off += 1
assert _off <= _CAND_ROWS
_CAND_VALID_ROWS = ([off + a for _, n, off in _CAND_BLOCKS for a in range(n)]
                    + [off for _, _, off in _CAND_SINGLES])


def _gelu(a):
    return 0.5 * a * (1.0 + lax.erf(a * SQRT_HALF))


def _group_mean_sq(y, ones_ref, group):
    sq = y * y
    hi = sq.astype(BF16)
    lo = (sq - hi.astype(F32)).astype(BF16)
    ones = ones_ref[...]
    tot = jnp.dot(hi, ones, preferred_element_type=F32) + jnp.dot(lo, ones, preferred_element_type=F32)
    return tot * (1.0 / group)


def _mixer_kernel(x_ref, g_attn_ref, w_in_ref, conv_w_ref, sgu_w_ref, sgu_bias_ref, g_v_ref,
                  g_oc_ref, g_os_ref, ones_c_ref, ones_s_ref, w_out_ref, h_ref,
                  tail_ref, mixed_ref, *, rows, seq_blocks, conv_width, sgu_width):
    i = pl.program_id(0)

    @pl.when(i % seq_blocks == 0)
    def _():
        tail_ref[...] = jnp.zeros_like(tail_ref)

    x = x_ref[...]
    xn = x * lax.rsqrt(jnp.mean(x * x, axis=-1, keepdims=True) + EPS) * g_attn_ref[...]
    proj = jnp.dot(xn.astype(BF16), w_in_ref[...], preferred_element_type=F32)

    c0, c1, c2, c3 = conv_width, 2 * conv_width, 3 * conv_width, 3 * conv_width + sgu_width
    b_gate, c_gate, hc = proj[:, :c0], proj[:, c0:c1], proj[:, c1:c2]

    z = c_gate * hc
    tail = tail_ref[...]
    row = lax.broadcasted_iota(jnp.int32, z.shape, 0)
    z_m1 = jnp.where(row == 0, tail[SUBLANES - 1:SUBLANES, :], pltpu.roll(z, 1, 0))
    z_m2 = jnp.where(row == 0, tail[SUBLANES - 2:SUBLANES - 1, :],
                     jnp.where(row == 1, tail[SUBLANES - 1:SUBLANES, :], pltpu.roll(z, 2, 0)))
    cw = conv_w_ref[...]
    conv = cw[0:1, :] * z_m2 + cw[1:2, :] * z_m1 + cw[2:3, :] * z
    y_conv = b_gate * conv
    tail_ref[...] = z[rows - SUBLANES:, :]

    zz = _gelu(proj[:, c2:])
    u, v = zz[:, :sgu_width], zz[:, sgu_width:]
    gw = sgu_width // SGU_GROUPS
    vn = v * lax.rsqrt(_group_mean_sq(v, ones_s_ref, gw) + EPS) * g_v_ref[...]
    vnb = vn.astype(BF16)
    ri = lax.broadcasted_iota(jnp.int32, (SGU_BLOCK, SGU_BLOCK), 0) // CHUNK
    ci = lax.broadcasted_iota(jnp.int32, (SGU_BLOCK, SGU_BLOCK), 1) // CHUNK
    causal = ri >= ci
    n_pos = rows // SGU_BLOCK
    for g in range(SGU_GROUPS):
        wg = jnp.where(causal, sgu_w_ref[g], 0.0).astype(BF16)
        cols = slice(g * gw, (g + 1) * gw)
        stacked = jnp.concatenate(
            [vnb[pb * SGU_BLOCK:(pb + 1) * SGU_BLOCK, cols] for pb in range(n_pos)], axis=1)
        res = jnp.dot(wg, stacked, preferred_element_type=F32)
        for pb in range(n_pos):
            mixed_ref[pb * SGU_BLOCK:(pb + 1) * SGU_BLOCK, cols] = (
                res[:, pb * gw:(pb + 1) * gw] + sgu_bias_ref[:, cols])
    y_sgu = u * mixed_ref[...]

    yc = y_conv * lax.rsqrt(_group_mean_sq(y_conv, ones_c_ref, conv_width // CONV_GROUPS) + EPS) * g_oc_ref[...]
    ys = y_sgu * lax.rsqrt(_group_mean_sq(y_sgu, ones_s_ref, gw) + EPS) * g_os_ref[...]
    y = jnp.concatenate([yc, ys], axis=1).astype(BF16)
    h_ref[...] = x + jnp.dot(y, w_out_ref[...], preferred_element_type=F32)


def _block_ones(width, group):
    idx = jnp.arange(width) // group
    return (idx[:, None] == idx[None, :]).astype(BF16)


def _mixer(h, seq, g_attn, w_in, conv_w, sgu_w, sgu_b, g_v, g_oc, g_os, w_out):
    n_tok, d_model = h.shape
    conv_width = conv_w.shape[1]
    sgu_width = g_v.shape[0]
    d_in = w_in.shape[1]
    rows = MIXER_ROWS
    assert seq % rows == 0 and rows % SGU_BLOCK == 0 and d_in == 3 * conv_width + 2 * sgu_width
    gw = sgu_width // SGU_GROUPS
    assert gw == SGU_BLOCK
    sgu_bias = jnp.repeat(sgu_b.T, gw, axis=1)
    const = lambda *shape: pl.BlockSpec(shape, lambda i: (0,) * len(shape))
    return pl.pallas_call(
        functools.partial(_mixer_kernel, rows=rows, seq_blocks=seq // rows,
                          conv_width=conv_width, sgu_width=sgu_width),
        grid=(n_tok // rows,),
        in_specs=[
            pl.BlockSpec((rows, d_model), lambda i: (i, 0)),
            const(1, d_model),
            const(d_model, d_in),
            const(CONV_K, conv_width),
            const(SGU_GROUPS, SGU_BLOCK, SGU_BLOCK),
            const(SGU_BLOCK, sgu_width),
            const(1, sgu_width),
            const(1, conv_width),
            const(1, sgu_width),
            const(conv_width, conv_width),
            const(sgu_width, sgu_width),
            const(conv_width + sgu_width, d_model),
        ],
        out_specs=pl.BlockSpec((rows, d_model), lambda i: (i, 0)),
        out_shape=jax.ShapeDtypeStruct((n_tok, d_model), F32),
        scratch_shapes=[pltpu.VMEM((SUBLANES, conv_width), F32),
                        pltpu.VMEM((rows, sgu_width), F32)],
        compiler_params=pltpu.CompilerParams(dimension_semantics=("arbitrary",),
                                             vmem_limit_bytes=MIXER_VMEM_BYTES),
        name="mixer",
    )(h, g_attn.reshape(1, -1), w_in.astype(BF16), conv_w, sgu_w, sgu_bias, g_v.reshape(1, -1),
      g_oc.reshape(1, -1), g_os.reshape(1, -1),
      _block_ones(conv_width, conv_width // CONV_GROUPS), _block_ones(sgu_width, gw),
      w_out.astype(BF16))


def _peer_kernel(h_ref, g_ffn_ref, wq_t_ref, keys_ref, u_ref, v_t_ref, g_fin_ref, out_ref,
                 h_t_ref, hn_t_ref, acc_ref, s1_ref, s2_ref, e1_ref, e2_ref, t16_ref,
                 act_ref, w_ref, top1_ref, top2_ref, cand_ref, *, tokens, n_chunks, final_norm):
    j = pl.program_id(1)
    lane_tiles = tokens // LANES

    def lane_bcast(g_ref):
        g = g_ref[...]
        return jnp.concatenate([g] * lane_tiles, axis=1)

    @pl.when(j == 0)
    def _route():
        h_t = h_ref[...].T
        h_t_ref[...] = h_t
        ms = jnp.mean(h_t * h_t, axis=0, keepdims=True)
        hn_t_ref[...] = (h_t * lax.rsqrt(ms + EPS) * lane_bcast(g_ffn_ref)).astype(BF16)
        acc_ref[...] = jnp.zeros_like(acc_ref)

        def head(hd, carry):
            for p, (s_ref, top_ref) in enumerate(((s1_ref, top1_ref), (s2_ref, top2_ref))):
                qrow = pl.multiple_of((hd * 2 + p) * N_KEYS, N_KEYS)
                q = jnp.dot(wq_t_ref[pl.ds(qrow, N_KEYS), :], hn_t_ref[...], preferred_element_type=F32)
                s = jnp.dot(keys_ref[hd * 2 + p], q.astype(BF16), preferred_element_type=F32)
                s_ref[hd] = s
                cur = s
                for k in range(PEER_TOPK):
                    m = jnp.max(cur, axis=0, keepdims=True)
                    top_ref[k:k + 1, :] = m
                    cur = jnp.where(cur == m, NEG_BIG, cur)

            cand_ref[...] = jnp.full(cand_ref.shape, NEG_BIG, F32)
            for b, n, off in _CAND_BLOCKS:
                cand_ref[off:off + n, :] = top1_ref[0:n, :] + top2_ref[b:b + 1, :]
            for a, b, off in _CAND_SINGLES:
                cand_ref[off:off + 1, :] = top1_ref[a:a + 1, :] + top2_ref[b:b + 1, :]
            cand = cand_ref[...]
            count = jnp.zeros(cand.shape, F32)
            for row in _CAND_VALID_ROWS:
                count = count + jnp.where(cand_ref[row:row + 1, :] > cand, 1.0, 0.0)
            t16 = jnp.min(jnp.where(count < float(PEER_TOPK), cand, POS_BIG), axis=0, keepdims=True)
            m1 = top1_ref[0:1, :]
            m2 = top2_ref[0:1, :]
            z = jnp.sum(jnp.where(cand >= t16, jnp.exp(cand - (m1 + m2)), 0.0), axis=0, keepdims=True)
            t16_ref[hd] = t16
            e1_ref[hd] = jnp.exp(s1_ref[hd] - m1)
            e2_ref[hd] = jnp.exp(s2_ref[hd] - m2) * (1.0 / z)
            return carry

        lax.fori_loop(0, PEER_HEADS, head, 0)

    act_ref[...] = jnp.dot(u_ref[...], hn_t_ref[...], preferred_element_type=F32)

    key_rows = pl.ds(pl.multiple_of(j * PEER_KEY_ROWS, PEER_KEY_ROWS), PEER_KEY_ROWS)

    def lane_tile(lt, carry):
        ls = pl.ds(pl.multiple_of(lt * LANES, LANES), LANES)
        for r in range(PEER_KEY_ROWS):
            gate = jnp.zeros((N_KEYS, LANES), F32)
            for hd in range(PEER_HEADS):
                s1_row = s1_ref[hd, key_rows, ls][r:r + 1, :]
                e1_row = e1_ref[hd, key_rows, ls][r:r + 1, :]
                pair = s2_ref[hd, :, ls] + s1_row
                sel = jnp.where(pair >= t16_ref[hd, :, ls], e2_ref[hd, :, ls], 0.0)
                gate = gate + sel * e1_row
            rows = slice(r * N_KEYS, (r + 1) * N_KEYS)
            w_ref[rows, ls] = (gate * _gelu(act_ref[rows, ls])).astype(BF16)
        return carry

    lax.fori_loop(0, lane_tiles, lane_tile, 0)
    acc_ref[...] += jnp.dot(v_t_ref[...], w_ref[...], preferred_element_type=F32)

    @pl.when(j == n_chunks - 1)
    def _finish():
        res = h_t_ref[...] + acc_ref[...]
        if final_norm:
            ms = jnp.mean(res * res, axis=0, keepdims=True)
            res = res * lax.rsqrt(ms + EPS) * lane_bcast(g_fin_ref)
        out_ref[...] = res.T


def _peer(h, g_ffn, w_q, sub_keys, expert_u, expert_v, g_final):
    n_tok, d_model = h.shape
    n_experts = expert_u.shape[0]
    d_half = sub_keys.shape[-1]
    assert sub_keys.shape[:3] == (PEER_HEADS, 2, N_KEYS) and d_half == N_KEYS
    assert n_experts == N_KEYS * N_KEYS and w_q.shape == (d_model, PEER_HEADS * 2 * d_half)
    tokens = PEER_TOKENS
    chunk = PEER_KEY_ROWS * N_KEYS
    n_chunks = n_experts // chunk
    assert n_tok % tokens == 0
    final_norm = g_final is not None
    g_fin = g_final if final_norm else jnp.ones((d_model,), F32)
    lanes = lambda g: jnp.broadcast_to(g.astype(F32)[:, None], (d_model, LANES))
    const = lambda *shape, **kw: pl.BlockSpec(shape, lambda i, j: (0,) * len(shape), **kw)
    table = lambda: pltpu.VMEM((PEER_HEADS, N_KEYS, tokens), F32)
    return pl.pallas_call(
        functools.partial(_peer_kernel, tokens=tokens, n_chunks=n_chunks, final_norm=final_norm),
        grid=(n_tok // tokens, n_chunks),
        in_specs=[
            pl.BlockSpec((tokens, d_model), lambda i, j: (i, 0)),
            const(d_model, LANES),
            const(PEER_HEADS * 2 * d_half, d_model, pipeline_mode=pl.Buffered(1)),
            const(PEER_HEADS * 2, N_KEYS, d_half),
            pl.BlockSpec((chunk, d_model), lambda i, j: (j, 0)),
            pl.BlockSpec((d_model, chunk), lambda i, j: (0, j)),
            const(d_model, LANES),
        ],
        out_specs=pl.BlockSpec((tokens, d_model), lambda i, j: (i, 0)),
        out_shape=jax.ShapeDtypeStruct((n_tok, d_model), F32),
        scratch_shapes=[
            pltpu.VMEM((d_model, tokens), F32),
            pltpu.VMEM((d_model, tokens), BF16),
            pltpu.VMEM((d_model, tokens), F32),
            table(), table(), table(), table(),
            pltpu.VMEM((PEER_HEADS, 1, tokens), F32),
            pltpu.VMEM((chunk, tokens), F32),
            pltpu.VMEM((chunk, tokens), BF16),
            pltpu.VMEM((PEER_TOPK, tokens), F32),
            pltpu.VMEM((PEER_TOPK, tokens), F32),
            pltpu.VMEM((_CAND_ROWS, tokens), F32),
        ],
        compiler_params=pltpu.CompilerParams(dimension_semantics=("arbitrary", "arbitrary"),
                                             vmem_limit_bytes=PEER_VMEM_BYTES),
        name="peer",
    )(h, lanes(g_ffn), w_q.T.astype(BF16),
      sub_keys.reshape(PEER_HEADS * 2, N_KEYS, d_half).astype(BF16),
      expert_u.astype(BF16), expert_v.T.astype(BF16), lanes(g_fin))


def kernel(x, attn_norm_g, w_in, conv_w, sgu_w, sgu_b, sgu_norm_g, out_norm_conv_g, out_norm_sgu_g,
           w_out, ffn_norm_g, peer_w_q, peer_sub_keys, peer_u, peer_v, final_norm_g):
    batch, seq, d_model = x.shape
    depth = attn_norm_g.shape[0]
    h = x.reshape(batch * seq, d_model)
    for l in range(depth):
        h = _mixer(h, seq, attn_norm_g[l], w_in[l], conv_w[l], sgu_w[l], sgu_b[l], sgu_norm_g[l],
                   out_norm_conv_g[l], out_norm_sgu_g[l], w_out[l])
        h = _peer(h, ffn_norm_g[l], peer_w_q[l], peer_sub_keys[l], peer_u[l], peer_v[l],
                  final_norm_g if l == depth - 1 else None)
    return h.reshape(batch, seq, d_model)
```

```python
import functools
import math

import jax
import jax.numpy as jnp
from jax import lax
from jax.experimental import pallas as pl
from jax.experimental.pallas import tpu as pltpu

F32 = jnp.float32
BF16 = jnp.bfloat16

EPS = 1e-6
CHUNK = 64
CONV_K = 3
CONV_GROUPS = 8
SGU_GROUPS = 4
SGU_BLOCK = 128
PEER_HEADS = 8
N_KEYS = 128
PEER_TOPK = 16
SQRT_HALF = math.sqrt(0.5)
NEG_BIG = -3.0e38
POS_BIG = 3.0e38

LANES = 128
SUBLANES = 8

MIXER_ROWS = 512
PEER_TOKENS = 512
PEER_KEY_ROWS = 8
MXU_PIECES = 4
GATE_ROWS = 4
GATE_KEYS = 32
MIXER_VMEM_BYTES = 48 * 1024 * 1024
PEER_VMEM_BYTES = 58 * 1024 * 1024

def _oddeven_merge(lo, hi, r):
    step = r * 2
    if step < hi - lo:
        yield from _oddeven_merge(lo, hi, step)
        yield from _oddeven_merge(lo + r, hi, step)
        yield from [(i, i + r) for i in range(lo + r, hi - r, step)]
    else:
        yield (lo, lo + r)


def _oddeven_merge_sort(lo, hi):
    if hi - lo >= 1:
        mid = lo + (hi - lo) // 2
        yield from _oddeven_merge_sort(lo, mid)
        yield from _oddeven_merge_sort(mid + 1, hi)
        yield from _oddeven_merge(lo, hi, 1)


_SORT_NET = tuple(_oddeven_merge_sort(0, PEER_TOPK - 1))
_BITONIC_NET = tuple((i, i | d) for d in (8, 4, 2, 1) for i in range(PEER_TOPK) if i & d == 0)
_PAIR_ROWS = tuple(PEER_TOPK // (b + 1) for b in range(PEER_TOPK))


def _apply_net(net, xs):
    xs = list(xs)
    for i, j in net:
        a, b = xs[i], xs[j]
        xs[i], xs[j] = jnp.maximum(a, b), jnp.minimum(a, b)
    return xs


def _merge_top(a, b):
    return _apply_net(_BITONIC_NET, [jnp.maximum(a[j], b[PEER_TOPK - 1 - j]) for j in range(PEER_TOPK)])


def _top16_and_next(xs):
    groups = [_apply_net(_SORT_NET, xs[g:g + PEER_TOPK]) for g in range(0, len(xs), PEER_TOPK)]
    while len(groups) > 1:
        groups = [_merge_top(groups[g], groups[g + 1]) for g in range(0, len(groups), 2)]
    top = groups[0]
    nxt = jnp.full(top[0].shape, NEG_BIG, F32)
    for x in xs:
        nxt = jnp.maximum(nxt, jnp.where(x < top[-1], x, NEG_BIG))
    return top, nxt


def _gelu(a):
    return 0.5 * a * (1.0 + lax.erf(a * SQRT_HALF))


def _group_mean_sq(y, ones_ref, group):
    tot = jnp.dot((y * y).astype(BF16), ones_ref[...], preferred_element_type=F32)
    return tot * (1.0 / group)


def _mixer_kernel(x_ref, g_attn_ref, w_in_ref, conv_w_ref, sgu_w_ref, sgu_bias_ref, g_v_ref,
                  g_oc_ref, g_os_ref, ones_c_ref, ones_s_ref, w_out_ref, h_ref,
                  tail_ref, mixed_ref, *, rows, seq_blocks, conv_width, sgu_width):
    i = pl.program_id(0)

    @pl.when(i % seq_blocks == 0)
    def _():
        tail_ref[...] = jnp.zeros_like(tail_ref)

    x = x_ref[...]
    xn = x * lax.rsqrt(jnp.mean(x * x, axis=-1, keepdims=True) + EPS) * g_attn_ref[...]
    proj = jnp.dot(xn.astype(BF16), w_in_ref[...], preferred_element_type=F32)

    c0, c1, c2, c3 = conv_width, 2 * conv_width, 3 * conv_width, 3 * conv_width + sgu_width
    b_gate, c_gate, hc = proj[:, :c0], proj[:, c0:c1], proj[:, c1:c2]

    z = c_gate * hc
    tail = tail_ref[...]
    row = lax.broadcasted_iota(jnp.int32, z.shape, 0)
    z_m1 = jnp.where(row == 0, tail[SUBLANES - 1:SUBLANES, :], pltpu.roll(z, 1, 0))
    z_m2 = jnp.where(row == 0, tail[SUBLANES - 2:SUBLANES - 1, :],
                     jnp.where(row == 1, tail[SUBLANES - 1:SUBLANES, :], pltpu.roll(z, 2, 0)))
    cw = conv_w_ref[...]
    conv = cw[0:1, :] * z_m2 + cw[1:2, :] * z_m1 + cw[2:3, :] * z
    y_conv = b_gate * conv
    tail_ref[...] = z[rows - SUBLANES:, :]

    zz = _gelu(proj[:, c2:])
    u, v = zz[:, :sgu_width], zz[:, sgu_width:]
    gw = sgu_width // SGU_GROUPS
    vn = v * lax.rsqrt(_group_mean_sq(v, ones_s_ref, gw) + EPS) * g_v_ref[...]
    vnb = vn.astype(BF16)
    ri = lax.broadcasted_iota(jnp.int32, (SGU_BLOCK, SGU_BLOCK), 0) // CHUNK
    ci = lax.broadcasted_iota(jnp.int32, (SGU_BLOCK, SGU_BLOCK), 1) // CHUNK
    causal = ri >= ci
    n_pos = rows // SGU_BLOCK
    for g in range(SGU_GROUPS):
        wg = jnp.where(causal, sgu_w_ref[g], 0.0).astype(BF16)
        cols = slice(g * gw, (g + 1) * gw)
        stacked = jnp.concatenate(
            [vnb[pb * SGU_BLOCK:(pb + 1) * SGU_BLOCK, cols] for pb in range(n_pos)], axis=1)
        res = jnp.dot(wg, stacked, preferred_element_type=F32)
        for pb in range(n_pos):
            mixed_ref[pb * SGU_BLOCK:(pb + 1) * SGU_BLOCK, cols] = (
                res[:, pb * gw:(pb + 1) * gw] + sgu_bias_ref[:, cols])
    y_sgu = u * mixed_ref[...]

    yc = y_conv * lax.rsqrt(_group_mean_sq(y_conv, ones_c_ref, conv_width // CONV_GROUPS) + EPS) * g_oc_ref[...]
    ys = y_sgu * lax.rsqrt(_group_mean_sq(y_sgu, ones_s_ref, gw) + EPS) * g_os_ref[...]
    y = jnp.concatenate([yc, ys], axis=1).astype(BF16)
    h_ref[...] = x + jnp.dot(y, w_out_ref[...], preferred_element_type=F32)


def _block_ones(width, group):
    idx = jnp.arange(width) // group
    return (idx[:, None] == idx[None, :]).astype(BF16)


def _mixer(h, seq, g_attn, w_in, conv_w, sgu_w, sgu_b, g_v, g_oc, g_os, w_out):
    n_tok, d_model = h.shape
    conv_width = conv_w.shape[1]
    sgu_width = g_v.shape[0]
    d_in = w_in.shape[1]
    rows = MIXER_ROWS
    assert seq % rows == 0 and rows % SGU_BLOCK == 0 and d_in == 3 * conv_width + 2 * sgu_width
    gw = sgu_width // SGU_GROUPS
    assert gw == SGU_BLOCK
    sgu_bias = jnp.repeat(sgu_b.T, gw, axis=1)
    const = lambda *shape: pl.BlockSpec(shape, lambda i: (0,) * len(shape))
    return pl.pallas_call(
        functools.partial(_mixer_kernel, rows=rows, seq_blocks=seq // rows,
                          conv_width=conv_width, sgu_width=sgu_width),
        grid=(n_tok // rows,),
        in_specs=[
            pl.BlockSpec((rows, d_model), lambda i: (i, 0)),
            const(1, d_model),
            const(d_model, d_in),
            const(CONV_K, conv_width),
            const(SGU_GROUPS, SGU_BLOCK, SGU_BLOCK),
            const(SGU_BLOCK, sgu_width),
            const(1, sgu_width),
            const(1, conv_width),
            const(1, sgu_width),
            const(conv_width, conv_width),
            const(sgu_width, sgu_width),
            const(conv_width + sgu_width, d_model),
        ],
        out_specs=pl.BlockSpec((rows, d_model), lambda i: (i, 0)),
        out_shape=jax.ShapeDtypeStruct((n_tok, d_model), F32),
        scratch_shapes=[pltpu.VMEM((SUBLANES, conv_width), F32),
                        pltpu.VMEM((rows, sgu_width), F32)],
        compiler_params=pltpu.CompilerParams(dimension_semantics=("arbitrary",),
                                             vmem_limit_bytes=MIXER_VMEM_BYTES),
        name="mixer",
    )(h, g_attn.reshape(1, -1), w_in.astype(BF16), conv_w, sgu_w, sgu_bias, g_v.reshape(1, -1),
      g_oc.reshape(1, -1), g_os.reshape(1, -1),
      _block_ones(conv_width, conv_width // CONV_GROUPS), _block_ones(sgu_width, gw),
      w_out.astype(BF16))


def _peer_kernel(h_ref, g_ffn_ref, wq_t_ref, keys_ref, kperm_ref, u_ref, v_t_ref, g_fin_ref, out_ref,
                 h_t_ref, hn_t_ref, acc_ref, s2_ref, e2_ref, theta_ref, e1_ref,
                 act0_ref, act1_ref, w0_ref, w1_ref, q_ref, sb0_ref, sb1_ref, top_ref, misc_ref,
                 *, tokens, n_chunks, final_norm):
    s = pl.program_id(1)
    lane_tiles = tokens // LANES
    act_refs = (act0_ref, act1_ref)
    w_refs = (w0_ref, w1_ref)
    chunk_rows = act0_ref.shape[0]
    d_model = acc_ref.shape[0]
    quarter_rows = chunk_rows // MXU_PIECES
    quarter_d = d_model // MXU_PIECES

    def lane_bcast(g_ref):
        g = g_ref[...]
        return jnp.concatenate([g] * lane_tiles, axis=1)

    def route():
        h_t = h_ref[...].T
        h_t_ref[...] = h_t
        ms = jnp.mean(h_t * h_t, axis=0, keepdims=True)
        hn_t_ref[:, 0:tokens] = (h_t * lax.rsqrt(ms + EPS) * lane_bcast(g_ffn_ref)).astype(BF16)
        acc_ref[:, 0:tokens] = jnp.zeros(h_t.shape, F32)

        n_q = PEER_HEADS * N_KEYS
        for p in (1, 0):
            q = jnp.dot(wq_t_ref[p * n_q:(p + 1) * n_q, :], hn_t_ref[:, 0:tokens], preferred_element_type=F32)
            q_ref[p] = q.astype(BF16)
        score_refs = (sb0_ref, sb1_ref)
        sb1_ref[...] = jnp.dot(kperm_ref[1], q_ref[1], preferred_element_type=F32)

        def sort_tile(p, ls):
            xs = [score_refs[p][k * PEER_HEADS:(k + 1) * PEER_HEADS, ls] for k in range(N_KEYS)]
            top, nxt = _top16_and_next(xs)
            for j, val in enumerate(top + [nxt]):
                top_ref[p, j * PEER_HEADS:(j + 1) * PEER_HEADS, ls] = val

        def lanes_of(lt):
            return pl.ds(pl.multiple_of(lt * LANES, LANES), LANES)

        def sort_half1(lt, carry):
            rows = pl.ds(pl.multiple_of(lt * (n_q // lane_tiles), n_q // lane_tiles), n_q // lane_tiles)
            sb0_ref[rows, :] = jnp.dot(kperm_ref[0, rows, :], q_ref[0], preferred_element_type=F32)
            sort_tile(1, lanes_of(lt))
            return carry

        def sort_half0(lt, carry):
            for k in range(PEER_HEADS // lane_tiles):
                hd = lt * (PEER_HEADS // lane_tiles) + k
                q_rows = pl.ds(pl.multiple_of(hd * N_KEYS, N_KEYS), N_KEYS)
                s2_ref[hd, :, 0:tokens] = jnp.dot(keys_ref[hd], q_ref[1, q_rows, :], preferred_element_type=F32)
            sort_tile(0, lanes_of(lt))
            return carry

        def thresholds(lt, carry):
            act_piece(0, lt)
            ls = lanes_of(lt)
            tops, mids = [], []
            for p in range(2):
                top = [top_ref[p, j * PEER_HEADS:(j + 1) * PEER_HEADS, ls] for j in range(PEER_TOPK + 1)]
                mids.append([0.5 * (top[j] + top[j + 1]) for j in range(PEER_TOPK)])
                tops.append(top[:PEER_TOPK])
            v1, v2 = tops
            pair = {(a, b): v1[a] + v2[b] for b in range(PEER_TOPK) for a in range(_PAIR_ROWS[b])}
            neg = jnp.full(v1[0].shape, NEG_BIG, F32)
            best = [pair[(a, 0)] for a in range(PEER_TOPK)]
            rest = [pair[(a, b)] for b in range(1, PEER_TOPK) for a in range(_PAIR_ROWS[b])]
            rest = rest + [neg] * (-len(rest) % PEER_TOPK)
            for g in range(0, len(rest), PEER_TOPK):
                best = _merge_top(best, _apply_net(_SORT_NET, rest[g:g + PEER_TOPK]))
            t16 = best[-1]
            z = jnp.exp(best[0] - best[0])
            for k in range(1, PEER_TOPK):
                z = z + jnp.exp(best[k] - best[0])
            thr = []
            for a in range(PEER_TOPK):
                cnt = jnp.zeros(t16.shape, F32)
                for b in range(PEER_TOPK // (a + 1)):
                    cnt = cnt + jnp.where(pair[(a, b)] >= t16, 1.0, 0.0)
                th = jnp.full(t16.shape, POS_BIG, F32)
                for j in range(PEER_TOPK // (a + 1)):
                    th = jnp.where(cnt >= float(j + 1), mids[1][j], th)
                thr.append(th)
            misc_ref[0:PEER_HEADS, ls] = v2[0]
            misc_ref[PEER_HEADS:2 * PEER_HEADS, ls] = 1.0 / z
            for k in range(N_KEYS):
                rows = slice(k * PEER_HEADS, (k + 1) * PEER_HEADS)
                x = sb0_ref[rows, ls]
                theta = jnp.full(x.shape, POS_BIG, F32)
                for a in reversed(range(PEER_TOPK)):
                    theta = jnp.where(x >= mids[0][a], thr[a], theta)
                theta_ref[lt, rows, :] = theta
                e1_ref[lt, rows, :] = SQRT_HALF * jnp.exp(x - v1[0])
            return carry

        lax.fori_loop(0, lane_tiles, sort_half1, 0)
        lax.fori_loop(0, lane_tiles, sort_half0, 0)
        lax.fori_loop(0, lane_tiles, thresholds, 0)

        for hd in range(PEER_HEADS):
            m2 = misc_ref[hd:hd + 1, :]
            z_inv = misc_ref[PEER_HEADS + hd:PEER_HEADS + hd + 1, :]
            e2_ref[hd, :, 0:tokens] = jnp.exp(s2_ref[hd, :, 0:tokens] - m2) * z_inv

    def act_piece(slot, mq):
        rows = pl.ds(pl.multiple_of(mq * quarter_rows, quarter_rows), quarter_rows)
        act_refs[slot][rows, 0:tokens] = jnp.dot(u_ref[rows, 0:d_model], hn_t_ref[:, 0:tokens],
                                                 preferred_element_type=F32)

    def out_piece(slot, mq):
        rows = slice(mq * quarter_d, (mq + 1) * quarter_d)
        acc_ref[rows, 0:tokens] += jnp.dot(v_t_ref[rows, 0:chunk_rows], w_refs[slot][:, 0:tokens],
                                           preferred_element_type=F32)

    def w_block(key_base, slot, r0, lt, q0):
        ls = slice(lt * LANES, (lt + 1) * LANES)
        qs = slice(q0, q0 + GATE_KEYS)
        gates = [jnp.zeros((GATE_KEYS, LANES), F32) for _ in range(GATE_ROWS)]
        for hd in range(PEER_HEADS):
            s2 = s2_ref[hd, qs, ls]
            e2 = e2_ref[hd, qs, ls]
            for k in range(GATE_ROWS):
                row = pl.ds(key_base + (r0 + k) * PEER_HEADS + hd, SUBLANES, stride=0)
                theta = jnp.concatenate([theta_ref[lt, row, :]] * (GATE_KEYS // SUBLANES), axis=0)
                e1 = jnp.concatenate([e1_ref[lt, row, :]] * (GATE_KEYS // SUBLANES), axis=0)
                gates[k] = gates[k] + jnp.where(s2 >= theta, e2, 0.0) * e1
        for k in range(GATE_ROWS):
            rows = slice((r0 + k) * N_KEYS + q0, (r0 + k) * N_KEYS + q0 + GATE_KEYS)
            x = act_refs[slot][rows, ls]
            w_refs[slot][rows, ls] = (gates[k] * (x * (1.0 + lax.erf(x)))).astype(BF16)

    def step(act_slot=None, build=None, out_slot=None):
        mxu = []
        for mq in range(MXU_PIECES):
            if act_slot is not None:
                mxu.append(functools.partial(act_piece, act_slot, mq))
            if out_slot is not None:
                mxu.append(functools.partial(out_piece, out_slot, mq))
        tiles = []
        if build is not None:
            chunk, slot = build
            key_base = pl.multiple_of(chunk * (PEER_KEY_ROWS * PEER_HEADS), PEER_KEY_ROWS * PEER_HEADS)
            tiles = [functools.partial(w_block, key_base, slot, r0, lt, q0)
                     for lt in range(lane_tiles)
                     for r0 in range(0, PEER_KEY_ROWS, GATE_ROWS)
                     for q0 in range(0, N_KEYS, GATE_KEYS)]
        n_groups = max(len(mxu), 1)
        per_group = -(-len(tiles) // n_groups)
        lead = per_group // 2 if mxu else 0
        for tile in tiles[:lead]:
            tile()
        for g in range(n_groups):
            if g < len(mxu):
                mxu[g]()
            for tile in tiles[lead + g * per_group:lead + (g + 1) * per_group]:
                tile()

    @pl.when(s == 0)
    def _():
        route()

    @pl.when(s == 1)
    def _():
        step(act_slot=1, build=(0, 0))

    for parity in (0, 1):
        @pl.when(jnp.logical_and(jnp.logical_and(s >= 2, s < n_chunks), s % 2 == parity))
        def _():
            step(act_slot=parity, build=(s - 1, 1 - parity), out_slot=parity)

    @pl.when(s == n_chunks)
    def _():
        step(build=(n_chunks - 1, (n_chunks - 1) % 2), out_slot=n_chunks % 2)

    @pl.when(s == n_chunks + 1)
    def _():
        step(out_slot=(n_chunks + 1) % 2)
        res = h_t_ref[...] + acc_ref[:, 0:tokens]
        if final_norm:
            ms = jnp.mean(res * res, axis=0, keepdims=True)
            res = res * lax.rsqrt(ms + EPS) * lane_bcast(g_fin_ref)
        out_ref[...] = res.T


def _expert_operands_kernel(u_ref, v_ref, u_out_ref, v_t_out_ref):
    chunk, d_model = u_ref.shape
    u_out_ref[:, 0:d_model] = (SQRT_HALF * u_ref[...]).astype(BF16)
    u_out_ref[:, d_model:] = jnp.zeros((chunk, LANES), BF16)
    v_t_out_ref[:, 0:chunk] = v_ref[...].T.astype(BF16)
    v_t_out_ref[:, chunk:] = jnp.zeros((d_model, LANES), BF16)


def _expert_operands(expert_u, expert_v, chunk):
    n_experts, d_model = expert_u.shape
    n_chunks = n_experts // chunk
    return pl.pallas_call(
        _expert_operands_kernel,
        grid=(n_chunks,),
        in_specs=[pl.BlockSpec((chunk, d_model), lambda c: (c, 0)),
                  pl.BlockSpec((chunk, d_model), lambda c: (c, 0))],
        out_specs=[pl.BlockSpec((chunk, d_model + LANES), lambda c: (c, 0)),
                   pl.BlockSpec((d_model, chunk + LANES), lambda c: (0, c))],
        out_shape=[jax.ShapeDtypeStruct((n_experts, d_model + LANES), BF16),
                   jax.ShapeDtypeStruct((d_model, n_chunks * (chunk + LANES)), BF16)],
        compiler_params=pltpu.CompilerParams(dimension_semantics=("arbitrary",),
                                             vmem_limit_bytes=MIXER_VMEM_BYTES),
        name="expert_operands",
    )(expert_u, expert_v)


def _peer(h, g_ffn, w_q, sub_keys, expert_u, expert_v, g_final):
    n_tok, d_model = h.shape
    n_experts = expert_u.shape[0]
    d_half = sub_keys.shape[-1]
    assert sub_keys.shape[:3] == (PEER_HEADS, 2, N_KEYS) and d_half == N_KEYS
    assert n_experts == N_KEYS * N_KEYS and w_q.shape == (d_model, PEER_HEADS * 2 * d_half)
    tokens = PEER_TOKENS
    chunk = PEER_KEY_ROWS * N_KEYS
    n_chunks = n_experts // chunk
    assert n_tok % tokens == 0 and n_chunks >= 3 and tokens // LANES == MXU_PIECES
    final_norm = g_final is not None
    wq_t = w_q.T.reshape(PEER_HEADS, 2, d_half, d_model).transpose(1, 0, 2, 3).reshape(-1, d_model).astype(BF16)
    keys_ph = sub_keys.transpose(1, 0, 2, 3).astype(BF16)
    keys = keys_ph[1]
    eye = jnp.eye(PEER_HEADS, dtype=BF16)
    kperm = (keys_ph.transpose(0, 2, 1, 3)[:, :, :, None, :] * eye[None, None, :, :, None]).reshape(
        2, N_KEYS * PEER_HEADS, PEER_HEADS * d_half)
    g_fin = g_final if final_norm else jnp.ones((d_model,), F32)
    lanes = lambda g: jnp.broadcast_to(g.astype(F32)[:, None], (d_model, LANES))
    const = lambda *shape, **kw: pl.BlockSpec(shape, lambda i, s: (0,) * len(shape), **kw)
    padded = tokens + LANES
    u_pad, v_t_pad = _expert_operands(expert_u, expert_v, chunk)
    table = lambda dtype: pltpu.VMEM((PEER_HEADS, N_KEYS, padded), dtype)
    return pl.pallas_call(
        functools.partial(_peer_kernel, tokens=tokens, n_chunks=n_chunks, final_norm=final_norm),
        grid=(n_tok // tokens, n_chunks + 2),
        in_specs=[
            pl.BlockSpec((tokens, d_model), lambda i, s: (i, 0), pipeline_mode=pl.Buffered(1)),
            const(d_model, LANES),
            const(PEER_HEADS * 2 * d_half, d_model, pipeline_mode=pl.Buffered(1)),
            const(PEER_HEADS, N_KEYS, d_half),
            const(2, PEER_HEADS * N_KEYS, PEER_HEADS * d_half, pipeline_mode=pl.Buffered(1)),
            pl.BlockSpec((chunk, d_model + LANES), lambda i, s: (jnp.minimum(s, n_chunks - 1), 0)),
            pl.BlockSpec((d_model, chunk + LANES), lambda i, s: (0, jnp.clip(s - 2, 0, n_chunks - 1))),
            const(d_model, LANES),
        ],
        out_specs=pl.BlockSpec((tokens, d_model), lambda i, s: (i, 0)),
        out_shape=jax.ShapeDtypeStruct((n_tok, d_model), F32),
        scratch_shapes=[
            pltpu.VMEM((d_model, tokens), F32),
            pltpu.VMEM((d_model, padded), BF16),
            pltpu.VMEM((d_model, padded), F32),
            table(F32),
            table(F32),
            pltpu.VMEM((tokens // LANES, N_KEYS * PEER_HEADS, LANES), F32),
            pltpu.VMEM((tokens // LANES, N_KEYS * PEER_HEADS, LANES), F32),
            pltpu.VMEM((chunk, padded), F32),
            pltpu.VMEM((chunk, padded), F32),
            pltpu.VMEM((chunk, padded), BF16),
            pltpu.VMEM((chunk, padded), BF16),
            pltpu.VMEM((2, PEER_HEADS * N_KEYS, tokens), BF16),
            pltpu.VMEM((PEER_HEADS * N_KEYS, tokens), F32),
            pltpu.VMEM((PEER_HEADS * N_KEYS, tokens), F32),
            pltpu.VMEM((2, (PEER_TOPK + 1) * PEER_HEADS, tokens), F32),
            pltpu.VMEM((2 * PEER_HEADS, tokens), F32),
        ],
        compiler_params=pltpu.CompilerParams(dimension_semantics=("arbitrary", "arbitrary"),
                                             vmem_limit_bytes=PEER_VMEM_BYTES),
        name="peer",
    )(h, lanes(g_ffn), wq_t, keys, kperm,
      u_pad, v_t_pad, lanes(g_fin))


def kernel(x, attn_norm_g, w_in, conv_w, sgu_w, sgu_b, sgu_norm_g, out_norm_conv_g, out_norm_sgu_g,
           w_out, ffn_norm_g, peer_w_q, peer_sub_keys, peer_u, peer_v, final_norm_g):
    batch, seq, d_model = x.shape
    depth = attn_norm_g.shape[0]
    h = x.reshape(batch * seq, d_model)
    for l in range(depth):
        h = _mixer(h, seq, attn_norm_g[l], w_in[l], conv_w[l], sgu_w[l], sgu_b[l], sgu_norm_g[l],
                   out_norm_conv_g[l], out_norm_sgu_g[l], w_out[l])
        h = _peer(h, ffn_norm_g[l], peer_w_q[l], peer_sub_keys[l], peer_u[l], peer_v[l],
                  final_norm_g if l == depth - 1 else None)
    return h.reshape(batch, seq, d_model)
```

```python
import functools
import math

import jax
import jax.numpy as jnp
from jax import lax
from jax.experimental import pallas as pl
from jax.experimental.pallas import tpu as pltpu

F32 = jnp.float32
BF16 = jnp.bfloat16

EPS = 1e-6
CHUNK = 64
CONV_K = 3
CONV_GROUPS = 8
SGU_GROUPS = 4
SGU_BLOCK = 128
PEER_HEADS = 8
N_KEYS = 128
PEER_TOPK = 16
SQRT_HALF = math.sqrt(0.5)
NEG_BIG = -3.0e38
POS_BIG = 3.0e38

LANES = 128
SUBLANES = 8

MIXER_ROWS = 512
PEER_TOKENS = 512
PEER_KEY_ROWS = 8
MXU_PIECES = 4
GATE_ROWS = 4
GATE_KEYS = 32
MIXER_VMEM_BYTES = 48 * 1024 * 1024
PEER_VMEM_BYTES = 58 * 1024 * 1024

def _oddeven_merge(lo, hi, r):
    step = r * 2
    if step < hi - lo:
        yield from _oddeven_merge(lo, hi, step)
        yield from _oddeven_merge(lo + r, hi, step)
        yield from [(i, i + r) for i in range(lo + r, hi - r, step)]
    else:
        yield (lo, lo + r)


def _oddeven_merge_sort(lo, hi):
    if hi - lo >= 1:
        mid = lo + (hi - lo) // 2
        yield from _oddeven_merge_sort(lo, mid)
        yield from _oddeven_merge_sort(mid + 1, hi)
        yield from _oddeven_merge(lo, hi, 1)


_SORT_NET = tuple(_oddeven_merge_sort(0, PEER_TOPK - 1))
_BITONIC_NET = tuple((i, i | d) for d in (8, 4, 2, 1) for i in range(PEER_TOPK) if i & d == 0)
_PAIR_ROWS = tuple(PEER_TOPK // (b + 1) for b in range(PEER_TOPK))


def _apply_net(net, xs):
    xs = list(xs)
    for i, j in net:
        a, b = xs[i], xs[j]
        xs[i], xs[j] = jnp.maximum(a, b), jnp.minimum(a, b)
    return xs


def _merge_top(a, b):
    return _apply_net(_BITONIC_NET, [jnp.maximum(a[j], b[PEER_TOPK - 1 - j]) for j in range(PEER_TOPK)])


def _top16_and_next(xs):
    groups = [_apply_net(_SORT_NET, xs[g:g + PEER_TOPK]) for g in range(0, len(xs), PEER_TOPK)]
    while len(groups) > 1:
        groups = [_merge_top(groups[g], groups[g + 1]) for g in range(0, len(groups), 2)]
    top = groups[0]
    nxt = jnp.full(top[0].shape, NEG_BIG, F32)
    for x in xs:
        nxt = jnp.maximum(nxt, jnp.where(x < top[-1], x, NEG_BIG))
    return top, nxt


def _gelu(a):
    return 0.5 * a * (1.0 + lax.erf(a * SQRT_HALF))


def _group_mean_sq(y, ones_ref, group):
    tot = jnp.dot((y * y).astype(BF16), ones_ref[...], preferred_element_type=F32)
    return tot * (1.0 / group)


def _mixer_kernel(x_ref, g_attn_ref, w_in_ref, conv_w_ref, sgu_w_ref, sgu_bias_ref, g_v_ref,
                  g_oc_ref, g_os_ref, ones_c_ref, ones_s_ref, w_out_ref, h_ref,
                  tail_ref, mixed_ref, *, rows, seq_blocks, conv_width, sgu_width):
    i = pl.program_id(0)

    @pl.when(i % seq_blocks == 0)
    def _():
        tail_ref[...] = jnp.zeros_like(tail_ref)

    x = x_ref[...]
    xn = x * lax.rsqrt(jnp.mean(x * x, axis=-1, keepdims=True) + EPS) * g_attn_ref[...]
    proj = jnp.dot(xn.astype(BF16), w_in_ref[...], preferred_element_type=F32)

    c0, c1, c2, c3 = conv_width, 2 * conv_width, 3 * conv_width, 3 * conv_width + sgu_width
    b_gate, c_gate, hc = proj[:, :c0], proj[:, c0:c1], proj[:, c1:c2]

    z = c_gate * hc
    tail = tail_ref[...]
    row = lax.broadcasted_iota(jnp.int32, z.shape, 0)
    z_m1 = jnp.where(row == 0, tail[SUBLANES - 1:SUBLANES, :], pltpu.roll(z, 1, 0))
    z_m2 = jnp.where(row == 0, tail[SUBLANES - 2:SUBLANES - 1, :],
                     jnp.where(row == 1, tail[SUBLANES - 1:SUBLANES, :], pltpu.roll(z, 2, 0)))
    cw = conv_w_ref[...]
    conv = cw[0:1, :] * z_m2 + cw[1:2, :] * z_m1 + cw[2:3, :] * z
    y_conv = b_gate * conv
    tail_ref[...] = z[rows - SUBLANES:, :]

    zz = _gelu(proj[:, c2:])
    u, v = zz[:, :sgu_width], zz[:, sgu_width:]
    gw = sgu_width // SGU_GROUPS
    vn = v * lax.rsqrt(_group_mean_sq(v, ones_s_ref, gw) + EPS) * g_v_ref[...]
    vnb = vn.astype(BF16)
    ri = lax.broadcasted_iota(jnp.int32, (SGU_BLOCK, SGU_BLOCK), 0) // CHUNK
    ci = lax.broadcasted_iota(jnp.int32, (SGU_BLOCK, SGU_BLOCK), 1) // CHUNK
    causal = ri >= ci
    n_pos = rows // SGU_BLOCK
    for g in range(SGU_GROUPS):
        wg = jnp.where(causal, sgu_w_ref[g], 0.0).astype(BF16)
        cols = slice(g * gw, (g + 1) * gw)
        stacked = jnp.concatenate(
            [vnb[pb * SGU_BLOCK:(pb + 1) * SGU_BLOCK, cols] for pb in range(n_pos)], axis=1)
        res = jnp.dot(wg, stacked, preferred_element_type=F32)
        for pb in range(n_pos):
            mixed_ref[pb * SGU_BLOCK:(pb + 1) * SGU_BLOCK, cols] = (
                res[:, pb * gw:(pb + 1) * gw] + sgu_bias_ref[:, cols])
    y_sgu = u * mixed_ref[...]

    yc = y_conv * lax.rsqrt(_group_mean_sq(y_conv, ones_c_ref, conv_width // CONV_GROUPS) + EPS) * g_oc_ref[...]
    ys = y_sgu * lax.rsqrt(_group_mean_sq(y_sgu, ones_s_ref, gw) + EPS) * g_os_ref[...]
    y = jnp.concatenate([yc, ys], axis=1).astype(BF16)
    h_ref[...] = x + jnp.dot(y, w_out_ref[...], preferred_element_type=F32)


def _block_ones(width, group):
    idx = jnp.arange(width) // group
    return (idx[:, None] == idx[None, :]).astype(BF16)


def _mixer(h, seq, g_attn, w_in, conv_w, sgu_w, sgu_b, g_v, g_oc, g_os, w_out):
    n_tok, d_model = h.shape
    conv_width = conv_w.shape[1]
    sgu_width = g_v.shape[0]
    d_in = w_in.shape[1]
    rows = MIXER_ROWS
    assert seq % rows == 0 and rows % SGU_BLOCK == 0 and d_in == 3 * conv_width + 2 * sgu_width
    gw = sgu_width // SGU_GROUPS
    assert gw == SGU_BLOCK
    sgu_bias = jnp.repeat(sgu_b.T, gw, axis=1)
    const = lambda *shape: pl.BlockSpec(shape, lambda i: (0,) * len(shape))
    return pl.pallas_call(
        functools.partial(_mixer_kernel, rows=rows, seq_blocks=seq // rows,
                          conv_width=conv_width, sgu_width=sgu_width),
        grid=(n_tok // rows,),
        in_specs=[
            pl.BlockSpec((rows, d_model), lambda i: (i, 0)),
            const(1, d_model),
            const(d_model, d_in),
            const(CONV_K, conv_width),
            const(SGU_GROUPS, SGU_BLOCK, SGU_BLOCK),
            const(SGU_BLOCK, sgu_width),
            const(1, sgu_width),
            const(1, conv_width),
            const(1, sgu_width),
            const(conv_width, conv_width),
            const(sgu_width, sgu_width),
            const(conv_width + sgu_width, d_model),
        ],
        out_specs=pl.BlockSpec((rows, d_model), lambda i: (i, 0)),
        out_shape=jax.ShapeDtypeStruct((n_tok, d_model), F32),
        scratch_shapes=[pltpu.VMEM((SUBLANES, conv_width), F32),
                        pltpu.VMEM((rows, sgu_width), F32)],
        compiler_params=pltpu.CompilerParams(dimension_semantics=("arbitrary",),
                                             vmem_limit_bytes=MIXER_VMEM_BYTES),
        name="mixer",
    )(h, g_attn.reshape(1, -1), w_in.astype(BF16), conv_w, sgu_w, sgu_bias, g_v.reshape(1, -1),
      g_oc.reshape(1, -1), g_os.reshape(1, -1),
      _block_ones(conv_width, conv_width // CONV_GROUPS), _block_ones(sgu_width, gw),
      w_out.astype(BF16))


def _peer_kernel(h_ref, g_ffn_ref, wq_t_ref, keys_ref, u_ref, v_t_ref, g_fin_ref, out_ref,
                 h_t_ref, hn_t_ref, acc_ref, s2_ref, e2_ref, theta_ref, e1_ref,
                 act0_ref, act1_ref, w0_ref, w1_ref, q_ref, sb0_ref, sb1_ref, top_ref, misc_ref,
                 *, tokens, n_chunks, final_norm):
    s = pl.program_id(1)
    lane_tiles = tokens // LANES
    act_refs = (act0_ref, act1_ref)
    w_refs = (w0_ref, w1_ref)
    chunk_rows = act0_ref.shape[0]
    d_model = acc_ref.shape[0]
    quarter_rows = chunk_rows // MXU_PIECES
    quarter_d = d_model // MXU_PIECES

    def lane_bcast(g_ref):
        g = g_ref[...]
        return jnp.concatenate([g] * lane_tiles, axis=1)

    def route():
        h_t = h_ref[...].T
        h_t_ref[...] = h_t
        ms = jnp.mean(h_t * h_t, axis=0, keepdims=True)
        hn_t_ref[:, 0:tokens] = (h_t * lax.rsqrt(ms + EPS) * lane_bcast(g_ffn_ref)).astype(BF16)
        acc_ref[:, 0:tokens] = jnp.zeros(h_t.shape, F32)

        n_q = PEER_HEADS * N_KEYS
        for p in (1, 0):
            q = jnp.dot(wq_t_ref[p * n_q:(p + 1) * n_q, :], hn_t_ref[:, 0:tokens], preferred_element_type=F32)
            q_ref[p] = q.astype(BF16)
        score_refs = (sb0_ref, sb1_ref)
        for p in (1, 0):
            for hd in range(PEER_HEADS):
                sc = jnp.dot(keys_ref[p * PEER_HEADS + hd], q_ref[p, hd * N_KEYS:(hd + 1) * N_KEYS, :],
                             preferred_element_type=F32)
                if p == 1:
                    s2_ref[hd, :, 0:tokens] = sc
                for lt in range(lane_tiles):
                    score_refs[p][lt, pl.ds(hd, N_KEYS, stride=PEER_HEADS), :] = sc[:, lt * LANES:(lt + 1) * LANES]

        def lanes_of(lt):
            return pl.ds(pl.multiple_of(lt * LANES, LANES), LANES)

        def sort_half(p):
            def body(lt, carry):
                xs = [score_refs[p][lt, k * PEER_HEADS:(k + 1) * PEER_HEADS, :] for k in range(N_KEYS)]
                top, nxt = _top16_and_next(xs)
                ls = lanes_of(lt)
                for j, val in enumerate(top + [nxt]):
                    top_ref[p, j * PEER_HEADS:(j + 1) * PEER_HEADS, ls] = val
                return carry
            return body

        def thresholds(lt, carry):
            act_piece(0, lt)
            ls = lanes_of(lt)
            tops, mids = [], []
            for p in range(2):
                top = [top_ref[p, j * PEER_HEADS:(j + 1) * PEER_HEADS, ls] for j in range(PEER_TOPK + 1)]
                mids.append([0.5 * (top[j] + top[j + 1]) for j in range(PEER_TOPK)])
                tops.append(top[:PEER_TOPK])
            v1, v2 = tops
            pair = {(a, b): v1[a] + v2[b] for b in range(PEER_TOPK) for a in range(_PAIR_ROWS[b])}
            neg = jnp.full(v1[0].shape, NEG_BIG, F32)
            best = [pair[(a, 0)] for a in range(PEER_TOPK)]
            rest = [pair[(a, b)] for b in range(1, PEER_TOPK) for a in range(_PAIR_ROWS[b])]
            rest = rest + [neg] * (-len(rest) % PEER_TOPK)
            for g in range(0, len(rest), PEER_TOPK):
                best = _merge_top(best, _apply_net(_SORT_NET, rest[g:g + PEER_TOPK]))
            t16 = best[-1]
            z = jnp.exp(best[0] - best[0])
            for k in range(1, PEER_TOPK):
                z = z + jnp.exp(best[k] - best[0])
            thr = []
            for a in range(PEER_TOPK):
                cnt = jnp.zeros(t16.shape, F32)
                for b in range(PEER_TOPK // (a + 1)):
                    cnt = cnt + jnp.where(pair[(a, b)] >= t16, 1.0, 0.0)
                th = jnp.full(t16.shape, POS_BIG, F32)
                for j in range(PEER_TOPK // (a + 1)):
                    th = jnp.where(cnt >= float(j + 1), mids[1][j], th)
                thr.append(th)
            misc_ref[0:PEER_HEADS, ls] = v2[0]
            misc_ref[PEER_HEADS:2 * PEER_HEADS, ls] = 1.0 / z
            for k in range(N_KEYS):
                rows = slice(k * PEER_HEADS, (k + 1) * PEER_HEADS)
                x = sb0_ref[lt, rows, :]
                theta = jnp.full(x.shape, POS_BIG, F32)
                for a in reversed(range(PEER_TOPK)):
                    theta = jnp.where(x >= mids[0][a], thr[a], theta)
                theta_ref[lt, rows, :] = theta
                e1_ref[lt, rows, :] = SQRT_HALF * jnp.exp(x - v1[0])
            return carry

        lax.fori_loop(0, lane_tiles, sort_half(1), 0)
        lax.fori_loop(0, lane_tiles, sort_half(0), 0)
        lax.fori_loop(0, lane_tiles, thresholds, 0)

        for hd in range(PEER_HEADS):
            m2 = misc_ref[hd:hd + 1, :]
            z_inv = misc_ref[PEER_HEADS + hd:PEER_HEADS + hd + 1, :]
            e2_ref[hd, :, 0:tokens] = jnp.exp(s2_ref[hd, :, 0:tokens] - m2) * z_inv

    def act_piece(slot, mq):
        rows = pl.ds(pl.multiple_of(mq * quarter_rows, quarter_rows), quarter_rows)
        act_refs[slot][rows, 0:tokens] = jnp.dot(u_ref[rows, 0:d_model], hn_t_ref[:, 0:tokens],
                                                 preferred_element_type=F32)

    def out_piece(slot, mq):
        rows = slice(mq * quarter_d, (mq + 1) * quarter_d)
        acc_ref[rows, 0:tokens] += jnp.dot(v_t_ref[rows, 0:chunk_rows], w_refs[slot][:, 0:tokens],
                                           preferred_element_type=F32)

    def w_block(key_base, slot, r0, lt, q0):
        ls = slice(lt * LANES, (lt + 1) * LANES)
        qs = slice(q0, q0 + GATE_KEYS)
        gates = [jnp.zeros((GATE_KEYS, LANES), F32) for _ in range(GATE_ROWS)]
        for hd in range(PEER_HEADS):
            s2 = s2_ref[hd, qs, ls]
            e2 = e2_ref[hd, qs, ls]
            for k in range(GATE_ROWS):
                row = pl.ds(key_base + (r0 + k) * PEER_HEADS + hd, 1)
                theta = jnp.broadcast_to(theta_ref[lt, row, :], (GATE_KEYS, LANES))
                e1 = jnp.broadcast_to(e1_ref[lt, row, :], (GATE_KEYS, LANES))
                gates[k] = gates[k] + jnp.where(s2 >= theta, e2, 0.0) * e1
        for k in range(GATE_ROWS):
            rows = slice((r0 + k) * N_KEYS + q0, (r0 + k) * N_KEYS + q0 + GATE_KEYS)
            x = act_refs[slot][rows, ls]
            w_refs[slot][rows, ls] = (gates[k] * (x * (1.0 + lax.erf(x)))).astype(BF16)

    def step(act_slot=None, build=None, out_slot=None):
        mxu = []
        for mq in range(MXU_PIECES):
            if act_slot is not None:
                mxu.append(functools.partial(act_piece, act_slot, mq))
            if out_slot is not None:
                mxu.append(functools.partial(out_piece, out_slot, mq))
        tiles = []
        if build is not None:
            chunk, slot = build
            key_base = pl.multiple_of(chunk * (PEER_KEY_ROWS * PEER_HEADS), PEER_KEY_ROWS * PEER_HEADS)
            tiles = [functools.partial(w_block, key_base, slot, r0, lt, q0)
                     for lt in range(lane_tiles)
                     for r0 in range(0, PEER_KEY_ROWS, GATE_ROWS)
                     for q0 in range(0, N_KEYS, GATE_KEYS)]
        n_groups = max(len(mxu), 1)
        per_group = -(-len(tiles) // n_groups)
        lead = per_group // 2 if mxu else 0
        for tile in tiles[:lead]:
            tile()
        for g in range(n_groups):
            if g < len(mxu):
                mxu[g]()
            for tile in tiles[lead + g * per_group:lead + (g + 1) * per_group]:
                tile()

    @pl.when(s == 0)
    def _():
        route()

    @pl.when(s == 1)
    def _():
        step(act_slot=1, build=(0, 0))

    for parity in (0, 1):
        @pl.when(jnp.logical_and(jnp.logical_and(s >= 2, s < n_chunks), s % 2 == parity))
        def _():
            step(act_slot=parity, build=(s - 1, 1 - parity), out_slot=parity)

    @pl.when(s == n_chunks)
    def _():
        step(build=(n_chunks - 1, (n_chunks - 1) % 2), out_slot=n_chunks % 2)

    @pl.when(s == n_chunks + 1)
    def _():
        step(out_slot=(n_chunks + 1) % 2)
        res = h_t_ref[...] + acc_ref[:, 0:tokens]
        if final_norm:
            ms = jnp.mean(res * res, axis=0, keepdims=True)
            res = res * lax.rsqrt(ms + EPS) * lane_bcast(g_fin_ref)
        out_ref[...] = res.T


def _expert_operands_kernel(u_ref, v_ref, u_out_ref, v_t_out_ref):
    chunk, d_model = u_ref.shape
    u_out_ref[:, 0:d_model] = (SQRT_HALF * u_ref[...]).astype(BF16)
    u_out_ref[:, d_model:] = jnp.zeros((chunk, LANES), BF16)
    v_t_out_ref[:, 0:chunk] = v_ref[...].T.astype(BF16)
    v_t_out_ref[:, chunk:] = jnp.zeros((d_model, LANES), BF16)


def _expert_operands(expert_u, expert_v, chunk):
    n_experts, d_model = expert_u.shape
    n_chunks = n_experts // chunk
    return pl.pallas_call(
        _expert_operands_kernel,
        grid=(n_chunks,),
        in_specs=[pl.BlockSpec((chunk, d_model), lambda c: (c, 0)),
                  pl.BlockSpec((chunk, d_model), lambda c: (c, 0))],
        out_specs=[pl.BlockSpec((chunk, d_model + LANES), lambda c: (c, 0)),
                   pl.BlockSpec((d_model, chunk + LANES), lambda c: (0, c))],
        out_shape=[jax.ShapeDtypeStruct((n_experts, d_model + LANES), BF16),
                   jax.ShapeDtypeStruct((d_model, n_chunks * (chunk + LANES)), BF16)],
        compiler_params=pltpu.CompilerParams(dimension_semantics=("arbitrary",),
                                             vmem_limit_bytes=MIXER_VMEM_BYTES),
        name="expert_operands",
    )(expert_u, expert_v)


def _peer(h, g_ffn, w_q, sub_keys, expert_u, expert_v, g_final):
    n_tok, d_model = h.shape
    n_experts = expert_u.shape[0]
    d_half = sub_keys.shape[-1]
    assert sub_keys.shape[:3] == (PEER_HEADS, 2, N_KEYS) and d_half == N_KEYS
    assert n_experts == N_KEYS * N_KEYS and w_q.shape == (d_model, PEER_HEADS * 2 * d_half)
    tokens = PEER_TOKENS
    chunk = PEER_KEY_ROWS * N_KEYS
    n_chunks = n_experts // chunk
    assert n_tok % tokens == 0 and n_chunks >= 3 and tokens // LANES == MXU_PIECES
    final_norm = g_final is not None
    wq_t = w_q.T.reshape(PEER_HEADS, 2, d_half, d_model).transpose(1, 0, 2, 3).reshape(-1, d_model).astype(BF16)
    keys = sub_keys.transpose(1, 0, 2, 3).reshape(2 * PEER_HEADS, N_KEYS, d_half).astype(BF16)
    g_fin = g_final if final_norm else jnp.ones((d_model,), F32)
    lanes = lambda g: jnp.broadcast_to(g.astype(F32)[:, None], (d_model, LANES))
    const = lambda *shape, **kw: pl.BlockSpec(shape, lambda i, s: (0,) * len(shape), **kw)
    padded = tokens + LANES
    u_pad, v_t_pad = _expert_operands(expert_u, expert_v, chunk)
    table = lambda dtype: pltpu.VMEM((PEER_HEADS, N_KEYS, padded), dtype)
    return pl.pallas_call(
        functools.partial(_peer_kernel, tokens=tokens, n_chunks=n_chunks, final_norm=final_norm),
        grid=(n_tok // tokens, n_chunks + 2),
        in_specs=[
            pl.BlockSpec((tokens, d_model), lambda i, s: (i, 0), pipeline_mode=pl.Buffered(1)),
            const(d_model, LANES),
            const(PEER_HEADS * 2 * d_half, d_model, pipeline_mode=pl.Buffered(1)),
            const(2 * PEER_HEADS, N_KEYS, d_half),
            pl.BlockSpec((chunk, d_model + LANES), lambda i, s: (jnp.minimum(s, n_chunks - 1), 0)),
            pl.BlockSpec((d_model, chunk + LANES), lambda i, s: (0, jnp.clip(s - 2, 0, n_chunks - 1))),
            const(d_model, LANES),
        ],
        out_specs=pl.BlockSpec((tokens, d_model), lambda i, s: (i, 0)),
        out_shape=jax.ShapeDtypeStruct((n_tok, d_model), F32),
        scratch_shapes=[
            pltpu.VMEM((d_model, tokens), F32),
            pltpu.VMEM((d_model, padded), BF16),
            pltpu.VMEM((d_model, padded), F32),
            table(F32),
            table(F32),
            pltpu.VMEM((tokens // LANES, N_KEYS * PEER_HEADS, LANES), F32),
            pltpu.VMEM((tokens // LANES, N_KEYS * PEER_HEADS, LANES), F32),
            pltpu.VMEM((chunk, padded), F32),
            pltpu.VMEM((chunk, padded), F32),
            pltpu.VMEM((chunk, padded), BF16),
            pltpu.VMEM((chunk, padded), BF16),
            pltpu.VMEM((2, PEER_HEADS * N_KEYS, tokens), BF16),
            pltpu.VMEM((tokens // LANES, PEER_HEADS * N_KEYS, LANES), F32),
            pltpu.VMEM((tokens // LANES, PEER_HEADS * N_KEYS, LANES), F32),
            pltpu.VMEM((2, (PEER_TOPK + 1) * PEER_HEADS, tokens), F32),
            pltpu.VMEM((2 * PEER_HEADS, tokens), F32),
        ],
        compiler_params=pltpu.CompilerParams(dimension_semantics=("arbitrary", "arbitrary"),
                                             vmem_limit_bytes=PEER_VMEM_BYTES),
        name="peer",
    )(h, lanes(g_ffn), wq_t, keys,
      u_pad, v_t_pad, lanes(g_fin))


def kernel(x, attn_norm_g, w_in, conv_w, sgu_w, sgu_b, sgu_norm_g, out_norm_conv_g, out_norm_sgu_g,
           w_out, ffn_norm_g, peer_w_q, peer_sub_keys, peer_u, peer_v, final_norm_g):
    batch, seq, d_model = x.shape
    depth = attn_norm_g.shape[0]
    h = x.reshape(batch * seq, d_model)
    for l in range(depth):
        h = _mixer(h, seq, attn_norm_g[l], w_in[l], conv_w[l], sgu_w[l], sgu_b[l], sgu_norm_g[l],
                   out_norm_conv_g[l], out_norm_sgu_g[l], w_out[l])
        h = _peer(h, ffn_norm_g[l], peer_w_q[l], peer_sub_keys[l], peer_u[l], peer_v[l],
                  final_norm_g if l == depth - 1 else None)
    return h.reshape(batch, seq, d_model)
```

```python
import functools
import math

import jax
import jax.numpy as jnp
from jax import lax
from jax.experimental import pallas as pl
from jax.experimental.pallas import tpu as pltpu

F32 = jnp.float32
BF16 = jnp.bfloat16

EPS = 1e-6
CHUNK = 64
CONV_K = 3
CONV_GROUPS = 8
SGU_GROUPS = 4
SGU_BLOCK = 128
PEER_HEADS = 8
N_KEYS = 128
PEER_TOPK = 16
SQRT_HALF = math.sqrt(0.5)
NEG_BIG = -3.0e38
POS_BIG = 3.0e38

LANES = 128
SUBLANES = 8

MIXER_ROWS = 512
PEER_TOKENS = 512
PEER_KEY_ROWS = 8
MXU_PIECES = 4
GATE_ROWS = 4
GATE_KEYS = 32
MIXER_VMEM_BYTES = 48 * 1024 * 1024
PEER_VMEM_BYTES = 58 * 1024 * 1024

def _oddeven_merge(lo, hi, r):
    step = r * 2
    if step < hi - lo:
        yield from _oddeven_merge(lo, hi, step)
        yield from _oddeven_merge(lo + r, hi, step)
        yield from [(i, i + r) for i in range(lo + r, hi - r, step)]
    else:
        yield (lo, lo + r)


def _oddeven_merge_sort(lo, hi):
    if hi - lo >= 1:
        mid = lo + (hi - lo) // 2
        yield from _oddeven_merge_sort(lo, mid)
        yield from _oddeven_merge_sort(mid + 1, hi)
        yield from _oddeven_merge(lo, hi, 1)


_SORT_NET = tuple(_oddeven_merge_sort(0, PEER_TOPK - 1))
_BITONIC_NET = tuple((i, i | d) for d in (8, 4, 2, 1) for i in range(PEER_TOPK) if i & d == 0)
_PAIR_ROWS = tuple(PEER_TOPK // (b + 1) for b in range(PEER_TOPK))


def _apply_net(net, xs):
    xs = list(xs)
    for i, j in net:
        a, b = xs[i], xs[j]
        xs[i], xs[j] = jnp.maximum(a, b), jnp.minimum(a, b)
    return xs


def _merge_top(a, b):
    return _apply_net(_BITONIC_NET, [jnp.maximum(a[j], b[PEER_TOPK - 1 - j]) for j in range(PEER_TOPK)])


def _top16_and_next(xs):
    groups = [_apply_net(_SORT_NET, xs[g:g + PEER_TOPK]) for g in range(0, len(xs), PEER_TOPK)]
    while len(groups) > 1:
        groups = [_merge_top(groups[g], groups[g + 1]) for g in range(0, len(groups), 2)]
    top = groups[0]
    nxt = jnp.full(top[0].shape, NEG_BIG, F32)
    for x in xs:
        nxt = jnp.maximum(nxt, jnp.where(x < top[-1], x, NEG_BIG))
    return top, nxt


def _gelu(a):
    return 0.5 * a * (1.0 + lax.erf(a * SQRT_HALF))


def _group_mean_sq(y, ones_ref, group):
    tot = jnp.dot((y * y).astype(BF16), ones_ref[...], preferred_element_type=F32)
    return tot * (1.0 / group)


def _mixer_kernel(x_ref, g_attn_ref, w_in_ref, conv_w_ref, sgu_w_ref, sgu_bias_ref, g_v_ref,
                  g_oc_ref, g_os_ref, ones_c_ref, ones_s_ref, w_out_ref, h_ref,
                  tail_ref, mixed_ref, *, rows, seq_blocks, conv_width, sgu_width):
    i = pl.program_id(0)

    @pl.when(i % seq_blocks == 0)
    def _():
        tail_ref[...] = jnp.zeros_like(tail_ref)

    x = x_ref[...]
    xn = x * lax.rsqrt(jnp.mean(x * x, axis=-1, keepdims=True) + EPS) * g_attn_ref[...]
    proj = jnp.dot(xn.astype(BF16), w_in_ref[...], preferred_element_type=F32)

    c0, c1, c2, c3 = conv_width, 2 * conv_width, 3 * conv_width, 3 * conv_width + sgu_width
    b_gate, c_gate, hc = proj[:, :c0], proj[:, c0:c1], proj[:, c1:c2]

    z = c_gate * hc
    tail = tail_ref[...]
    row = lax.broadcasted_iota(jnp.int32, z.shape, 0)
    z_m1 = jnp.where(row == 0, tail[SUBLANES - 1:SUBLANES, :], pltpu.roll(z, 1, 0))
    z_m2 = jnp.where(row == 0, tail[SUBLANES - 2:SUBLANES - 1, :],
                     jnp.where(row == 1, tail[SUBLANES - 1:SUBLANES, :], pltpu.roll(z, 2, 0)))
    cw = conv_w_ref[...]
    conv = cw[0:1, :] * z_m2 + cw[1:2, :] * z_m1 + cw[2:3, :] * z
    y_conv = b_gate * conv
    tail_ref[...] = z[rows - SUBLANES:, :]

    zz = _gelu(proj[:, c2:])
    u, v = zz[:, :sgu_width], zz[:, sgu_width:]
    gw = sgu_width // SGU_GROUPS
    vn = v * lax.rsqrt(_group_mean_sq(v, ones_s_ref, gw) + EPS) * g_v_ref[...]
    vnb = vn.astype(BF16)
    ri = lax.broadcasted_iota(jnp.int32, (SGU_BLOCK, SGU_BLOCK), 0) // CHUNK
    ci = lax.broadcasted_iota(jnp.int32, (SGU_BLOCK, SGU_BLOCK), 1) // CHUNK
    causal = ri >= ci
    n_pos = rows // SGU_BLOCK
    for g in range(SGU_GROUPS):
        wg = jnp.where(causal, sgu_w_ref[g], 0.0).astype(BF16)
        cols = slice(g * gw, (g + 1) * gw)
        stacked = jnp.concatenate(
            [vnb[pb * SGU_BLOCK:(pb + 1) * SGU_BLOCK, cols] for pb in range(n_pos)], axis=1)
        res = jnp.dot(wg, stacked, preferred_element_type=F32)
        for pb in range(n_pos):
            mixed_ref[pb * SGU_BLOCK:(pb + 1) * SGU_BLOCK, cols] = (
                res[:, pb * gw:(pb + 1) * gw] + sgu_bias_ref[:, cols])
    y_sgu = u * mixed_ref[...]

    yc = y_conv * lax.rsqrt(_group_mean_sq(y_conv, ones_c_ref, conv_width // CONV_GROUPS) + EPS) * g_oc_ref[...]
    ys = y_sgu * lax.rsqrt(_group_mean_sq(y_sgu, ones_s_ref, gw) + EPS) * g_os_ref[...]
    y = jnp.concatenate([yc, ys], axis=1).astype(BF16)
    h_ref[...] = x + jnp.dot(y, w_out_ref[...], preferred_element_type=F32)


def _block_ones(width, group):
    idx = jnp.arange(width) // group
    return (idx[:, None] == idx[None, :]).astype(BF16)


def _mixer(h, seq, g_attn, w_in, conv_w, sgu_w, sgu_b, g_v, g_oc, g_os, w_out):
    n_tok, d_model = h.shape
    conv_width = conv_w.shape[1]
    sgu_width = g_v.shape[0]
    d_in = w_in.shape[1]
    rows = MIXER_ROWS
    assert seq % rows == 0 and rows % SGU_BLOCK == 0 and d_in == 3 * conv_width + 2 * sgu_width
    gw = sgu_width // SGU_GROUPS
    assert gw == SGU_BLOCK
    sgu_bias = jnp.repeat(sgu_b.T, gw, axis=1)
    const = lambda *shape: pl.BlockSpec(shape, lambda i: (0,) * len(shape))
    return pl.pallas_call(
        functools.partial(_mixer_kernel, rows=rows, seq_blocks=seq // rows,
                          conv_width=conv_width, sgu_width=sgu_width),
        grid=(n_tok // rows,),
        in_specs=[
            pl.BlockSpec((rows, d_model), lambda i: (i, 0)),
            const(1, d_model),
            const(d_model, d_in),
            const(CONV_K, conv_width),
            const(SGU_GROUPS, SGU_BLOCK, SGU_BLOCK),
            const(SGU_BLOCK, sgu_width),
            const(1, sgu_width),
            const(1, conv_width),
            const(1, sgu_width),
            const(conv_width, conv_width),
            const(sgu_width, sgu_width),
            const(conv_width + sgu_width, d_model),
        ],
        out_specs=pl.BlockSpec((rows, d_model), lambda i: (i, 0)),
        out_shape=jax.ShapeDtypeStruct((n_tok, d_model), F32),
        scratch_shapes=[pltpu.VMEM((SUBLANES, conv_width), F32),
                        pltpu.VMEM((rows, sgu_width), F32)],
        compiler_params=pltpu.CompilerParams(dimension_semantics=("arbitrary",),
                                             vmem_limit_bytes=MIXER_VMEM_BYTES),
        name="mixer",
    )(h, g_attn.reshape(1, -1), w_in.astype(BF16), conv_w, sgu_w, sgu_bias, g_v.reshape(1, -1),
      g_oc.reshape(1, -1), g_os.reshape(1, -1),
      _block_ones(conv_width, conv_width // CONV_GROUPS), _block_ones(sgu_width, gw),
      w_out.astype(BF16))


def _peer_kernel(h_ref, g_ffn_ref, wq_t_ref, keys_ref, u_ref, v_t_ref, g_fin_ref, out_ref,
                 h_t_ref, hn_t_ref, acc_ref, s2_ref, e2_ref, theta_ref, e1_ref,
                 act0_ref, act1_ref, w0_ref, w1_ref, q_ref, sb0_ref, sb1_ref, top_ref, misc_ref,
                 *, tokens, n_chunks, final_norm):
    s = pl.program_id(1)
    lane_tiles = tokens // LANES
    act_refs = (act0_ref, act1_ref)
    w_refs = (w0_ref, w1_ref)
    chunk_rows = act0_ref.shape[0]
    d_model = acc_ref.shape[0]
    quarter_rows = chunk_rows // MXU_PIECES
    quarter_d = d_model // MXU_PIECES

    def lane_bcast(g_ref):
        g = g_ref[...]
        return jnp.concatenate([g] * lane_tiles, axis=1)

    def route():
        h_t = h_ref[...].T
        h_t_ref[...] = h_t
        ms = jnp.mean(h_t * h_t, axis=0, keepdims=True)
        hn_t_ref[:, 0:tokens] = (h_t * lax.rsqrt(ms + EPS) * lane_bcast(g_ffn_ref)).astype(BF16)
        acc_ref[:, 0:tokens] = jnp.zeros(h_t.shape, F32)

        n_q = PEER_HEADS * N_KEYS
        for p in (1, 0):
            q = jnp.dot(wq_t_ref[p * n_q:(p + 1) * n_q, :], hn_t_ref[:, 0:tokens], preferred_element_type=F32)
            q_ref[p] = q.astype(BF16)
        score_refs = (sb0_ref, sb1_ref)
        for p in (1, 0):
            for hd in range(PEER_HEADS):
                sc = jnp.dot(keys_ref[p * PEER_HEADS + hd], q_ref[p, hd * N_KEYS:(hd + 1) * N_KEYS, :],
                             preferred_element_type=F32)
                if p == 1:
                    s2_ref[hd, :, 0:tokens] = sc
                for lt in range(lane_tiles):
                    score_refs[p][lt, pl.ds(hd, N_KEYS, stride=PEER_HEADS), :] = sc[:, lt * LANES:(lt + 1) * LANES]

        def lanes_of(lt):
            return pl.ds(pl.multiple_of(lt * LANES, LANES), LANES)

        def sort_half(p):
            def body(lt, carry):
                xs = [score_refs[p][lt, k * PEER_HEADS:(k + 1) * PEER_HEADS, :] for k in range(N_KEYS)]
                top, nxt = _top16_and_next(xs)
                ls = lanes_of(lt)
                for j, val in enumerate(top + [nxt]):
                    top_ref[p, j * PEER_HEADS:(j + 1) * PEER_HEADS, ls] = val
                return carry
            return body

        def thresholds(lt, carry):
            act_piece(0, lt)
            ls = lanes_of(lt)
            tops, mids = [], []
            for p in range(2):
                top = [top_ref[p, j * PEER_HEADS:(j + 1) * PEER_HEADS, ls] for j in range(PEER_TOPK + 1)]
                mids.append([0.5 * (top[j] + top[j + 1]) for j in range(PEER_TOPK)])
                tops.append(top[:PEER_TOPK])
            v1, v2 = tops
            pair = {(a, b): v1[a] + v2[b] for b in range(PEER_TOPK) for a in range(_PAIR_ROWS[b])}
            neg = jnp.full(v1[0].shape, NEG_BIG, F32)
            best = [pair[(a, 0)] for a in range(PEER_TOPK)]
            rest = [pair[(a, b)] for b in range(1, PEER_TOPK) for a in range(_PAIR_ROWS[b])]
            rest = rest + [neg] * (-len(rest) % PEER_TOPK)
            for g in range(0, len(rest), PEER_TOPK):
                best = _merge_top(best, _apply_net(_SORT_NET, rest[g:g + PEER_TOPK]))
            t16 = best[-1]
            z = jnp.exp(best[0] - best[0])
            for k in range(1, PEER_TOPK):
                z = z + jnp.exp(best[k] - best[0])
            thr = []
            for a in range(PEER_TOPK):
                cnt = jnp.zeros(t16.shape, F32)
                for b in range(PEER_TOPK // (a + 1)):
                    cnt = cnt + jnp.where(pair[(a, b)] >= t16, 1.0, 0.0)
                th = jnp.full(t16.shape, POS_BIG, F32)
                for j in range(PEER_TOPK // (a + 1)):
                    th = jnp.where(cnt >= float(j + 1), mids[1][j], th)
                thr.append(th)
            misc_ref[0:PEER_HEADS, ls] = v2[0]
            misc_ref[PEER_HEADS:2 * PEER_HEADS, ls] = 1.0 / z
            for k in range(N_KEYS):
                rows = slice(k * PEER_HEADS, (k + 1) * PEER_HEADS)
                x = sb0_ref[lt, rows, :]
                theta = jnp.full(x.shape, POS_BIG, F32)
                for a in reversed(range(PEER_TOPK)):
                    theta = jnp.where(x >= mids[0][a], thr[a], theta)
                theta_ref[lt, rows, :] = theta
                e1_ref[lt, rows, :] = SQRT_HALF * jnp.exp(x - v1[0])
            return carry

        lax.fori_loop(0, lane_tiles, sort_half(1), 0)
        lax.fori_loop(0, lane_tiles, sort_half(0), 0)
        lax.fori_loop(0, lane_tiles, thresholds, 0)

        for hd in range(PEER_HEADS):
            m2 = misc_ref[hd:hd + 1, :]
            z_inv = misc_ref[PEER_HEADS + hd:PEER_HEADS + hd + 1, :]
            e2_ref[hd, :, 0:tokens] = jnp.exp(s2_ref[hd, :, 0:tokens] - m2) * z_inv

    def act_piece(slot, mq):
        rows = pl.ds(pl.multiple_of(mq * quarter_rows, quarter_rows), quarter_rows)
        act_refs[slot][rows, 0:tokens] = jnp.dot(u_ref[rows, 0:d_model], hn_t_ref[:, 0:tokens],
                                                 preferred_element_type=F32)

    def out_piece(slot, mq):
        rows = slice(mq * quarter_d, (mq + 1) * quarter_d)
        acc_ref[rows, 0:tokens] += jnp.dot(v_t_ref[rows, 0:chunk_rows], w_refs[slot][:, 0:tokens],
                                           preferred_element_type=F32)

    def w_block(key_base, slot, r0, lt, q0):
        ls = slice(lt * LANES, (lt + 1) * LANES)
        qs = slice(q0, q0 + GATE_KEYS)
        gates = [jnp.zeros((GATE_KEYS, LANES), F32) for _ in range(GATE_ROWS)]
        for hd in range(PEER_HEADS):
            s2 = s2_ref[hd, qs, ls]
            e2 = e2_ref[hd, qs, ls]
            for k in range(GATE_ROWS):
                row = pl.ds(key_base + (r0 + k) * PEER_HEADS + hd, 1)
                theta = jnp.broadcast_to(theta_ref[lt, row, :], (GATE_KEYS, LANES))
                e1 = jnp.broadcast_to(e1_ref[lt, row, :], (GATE_KEYS, LANES))
                gates[k] = gates[k] + jnp.where(s2 >= theta, e2, 0.0) * e1
        for k in range(GATE_ROWS):
            rows = slice((r0 + k) * N_KEYS + q0, (r0 + k) * N_KEYS + q0 + GATE_KEYS)
            x = act_refs[slot][rows, ls]
            w_refs[slot][rows, ls] = (gates[k] * (x * (1.0 + lax.erf(x)))).astype(BF16)

    def step(act_slot=None, build=None, out_slot=None):
        mxu = []
        for mq in range(MXU_PIECES):
            if act_slot is not None:
                mxu.append(functools.partial(act_piece, act_slot, mq))
            if out_slot is not None:
                mxu.append(functools.partial(out_piece, out_slot, mq))
        tiles = []
        if build is not None:
            chunk, slot = build
            key_base = pl.multiple_of(chunk * (PEER_KEY_ROWS * PEER_HEADS), PEER_KEY_ROWS * PEER_HEADS)
            tiles = [functools.partial(w_block, key_base, slot, r0, lt, q0)
                     for lt in range(lane_tiles)
                     for r0 in range(0, PEER_KEY_ROWS, GATE_ROWS)
                     for q0 in range(0, N_KEYS, GATE_KEYS)]
        n_groups = max(len(mxu), 1)
        per_group = -(-len(tiles) // n_groups)
        lead = per_group // 2 if mxu else 0
        for tile in tiles[:lead]:
            tile()
        for g in range(n_groups):
            if g < len(mxu):
                mxu[g]()
            for tile in tiles[lead + g * per_group:lead + (g + 1) * per_group]:
                tile()

    @pl.when(s == 0)
    def _():
        route()

    @pl.when(s == 1)
    def _():
        step(act_slot=1, build=(0, 0))

    for parity in (0, 1):
        @pl.when(jnp.logical_and(jnp.logical_and(s >= 2, s < n_chunks), s % 2 == parity))
        def _():
            step(act_slot=parity, build=(s - 1, 1 - parity), out_slot=parity)

    @pl.when(s == n_chunks)
    def _():
        step(build=(n_chunks - 1, (n_chunks - 1) % 2), out_slot=n_chunks % 2)

    @pl.when(s == n_chunks + 1)
    def _():
        step(out_slot=(n_chunks + 1) % 2)
        res = h_t_ref[...] + acc_ref[:, 0:tokens]
        if final_norm:
            ms = jnp.mean(res * res, axis=0, keepdims=True)
            res = res * lax.rsqrt(ms + EPS) * lane_bcast(g_fin_ref)
        out_ref[...] = res.T


def _expert_operands_kernel(u_ref, v_ref, u_out_ref, v_t_out_ref):
    chunk, d_model = u_ref.shape
    u_out_ref[:, 0:d_model] = (SQRT_HALF * u_ref[...]).astype(BF16)
    u_out_ref[:, d_model:] = jnp.zeros((chunk, LANES), BF16)
    v_t_out_ref[:, 0:chunk] = v_ref[...].T.astype(BF16)
    v_t_out_ref[:, chunk:] = jnp.zeros((d_model, LANES), BF16)


def _expert_operands(expert_u, expert_v, chunk):
    n_experts, d_model = expert_u.shape
    n_chunks = n_experts // chunk
    return pl.pallas_call(
        _expert_operands_kernel,
        grid=(n_chunks,),
        in_specs=[pl.BlockSpec((chunk, d_model), lambda c: (c, 0)),
                  pl.BlockSpec((chunk, d_model), lambda c: (c, 0))],
        out_specs=[pl.BlockSpec((chunk, d_model + LANES), lambda c: (c, 0)),
                   pl.BlockSpec((d_model, chunk + LANES), lambda c: (0, c))],
        out_shape=[jax.ShapeDtypeStruct((n_experts, d_model + LANES), BF16),
                   jax.ShapeDtypeStruct((d_model, n_chunks * (chunk + LANES)), BF16)],
        compiler_params=pltpu.CompilerParams(dimension_semantics=("arbitrary",),
                                             vmem_limit_bytes=MIXER_VMEM_BYTES),
        name="expert_operands",
    )(expert_u, expert_v)


def _query_operand_kernel(w_ref, w_t_ref):
    w_t_ref[...] = w_ref[...].T.astype(BF16)


def _query_operand(w_q, d_half):
    d_model, n_cols = w_q.shape
    n_blocks = n_cols // d_half
    return pl.pallas_call(
        _query_operand_kernel,
        grid=(n_blocks,),
        in_specs=[pl.BlockSpec((d_model, d_half), lambda c: (0, c))],
        out_specs=pl.BlockSpec((d_half, d_model), lambda c: ((c % 2) * (n_blocks // 2) + c // 2, 0)),
        out_shape=jax.ShapeDtypeStruct((n_cols, d_model), BF16),
        compiler_params=pltpu.CompilerParams(dimension_semantics=("arbitrary",)),
        name="query_operand",
    )(w_q)


def _peer(h, g_ffn, w_q, sub_keys, expert_u, expert_v, g_final):
    n_tok, d_model = h.shape
    n_experts = expert_u.shape[0]
    d_half = sub_keys.shape[-1]
    assert sub_keys.shape[:3] == (PEER_HEADS, 2, N_KEYS) and d_half == N_KEYS
    assert n_experts == N_KEYS * N_KEYS and w_q.shape == (d_model, PEER_HEADS * 2 * d_half)
    tokens = PEER_TOKENS
    chunk = PEER_KEY_ROWS * N_KEYS
    n_chunks = n_experts // chunk
    assert n_tok % tokens == 0 and n_chunks >= 3 and tokens // LANES == MXU_PIECES
    final_norm = g_final is not None
    wq_t = _query_operand(w_q, d_half)
    keys = sub_keys.transpose(1, 0, 2, 3).reshape(2 * PEER_HEADS, N_KEYS, d_half).astype(BF16)
    g_fin = g_final if final_norm else jnp.ones((d_model,), F32)
    lanes = lambda g: jnp.broadcast_to(g.astype(F32)[:, None], (d_model, LANES))
    const = lambda *shape, **kw: pl.BlockSpec(shape, lambda i, s: (0,) * len(shape), **kw)
    padded = tokens + LANES
    u_pad, v_t_pad = _expert_operands(expert_u, expert_v, chunk)
    table = lambda dtype: pltpu.VMEM((PEER_HEADS, N_KEYS, padded), dtype)
    return pl.pallas_call(
        functools.partial(_peer_kernel, tokens=tokens, n_chunks=n_chunks, final_norm=final_norm),
        grid=(n_tok // tokens, n_chunks + 2),
        in_specs=[
            pl.BlockSpec((tokens, d_model), lambda i, s: (i, 0), pipeline_mode=pl.Buffered(1)),
            const(d_model, LANES),
            const(PEER_HEADS * 2 * d_half, d_model, pipeline_mode=pl.Buffered(1)),
            const(2 * PEER_HEADS, N_KEYS, d_half),
            pl.BlockSpec((chunk, d_model + LANES), lambda i, s: (jnp.minimum(s, n_chunks - 1), 0)),
            pl.BlockSpec((d_model, chunk + LANES), lambda i, s: (0, jnp.clip(s - 2, 0, n_chunks - 1))),
            const(d_model, LANES),
        ],
        out_specs=pl.BlockSpec((tokens, d_model), lambda i, s: (i, 0)),
        out_shape=jax.ShapeDtypeStruct((n_tok, d_model), F32),
        scratch_shapes=[
            pltpu.VMEM((d_model, tokens), F32),
            pltpu.VMEM((d_model, padded), BF16),
            pltpu.VMEM((d_model, padded), F32),
            table(F32),
            table(F32),
            pltpu.VMEM((tokens // LANES, N_KEYS * PEER_HEADS, LANES), F32),
            pltpu.VMEM((tokens // LANES, N_KEYS * PEER_HEADS, LANES), F32),
            pltpu.VMEM((chunk, padded), F32),
            pltpu.VMEM((chunk, padded), F32),
            pltpu.VMEM((chunk, padded), BF16),
            pltpu.VMEM((chunk, padded), BF16),
            pltpu.VMEM((2, PEER_HEADS * N_KEYS, tokens), BF16),
            pltpu.VMEM((tokens // LANES, PEER_HEADS * N_KEYS, LANES), F32),
            pltpu.VMEM((tokens // LANES, PEER_HEADS * N_KEYS, LANES), F32),
            pltpu.VMEM((2, (PEER_TOPK + 1) * PEER_HEADS, tokens), F32),
            pltpu.VMEM((2 * PEER_HEADS, tokens), F32),
        ],
        compiler_params=pltpu.CompilerParams(dimension_semantics=("arbitrary", "arbitrary"),
                                             vmem_limit_bytes=PEER_VMEM_BYTES),
        name="peer",
    )(h, lanes(g_ffn), wq_t, keys,
      u_pad, v_t_pad, lanes(g_fin))


def kernel(x, attn_norm_g, w_in, conv_w, sgu_w, sgu_b, sgu_norm_g, out_norm_conv_g, out_norm_sgu_g,
           w_out, ffn_norm_g, peer_w_q, peer_sub_keys, peer_u, peer_v, final_norm_g):
    batch, seq, d_model = x.shape
    depth = attn_norm_g.shape[0]
    h = x.reshape(batch * seq, d_model)
    for l in range(depth):
        h = _mixer(h, seq, attn_norm_g[l], w_in[l], conv_w[l], sgu_w[l], sgu_b[l], sgu_norm_g[l],
                   out_norm_conv_g[l], out_norm_sgu_g[l], w_out[l])
        h = _peer(h, ffn_norm_g[l], peer_w_q[l], peer_sub_keys[l], peer_u[l], peer_v[l],
                  final_norm_g if l == depth - 1 else None)
    return h.reshape(batch, seq, d_model)
```

```python
import functools
import math

import jax
import jax.numpy as jnp
from jax import lax
from jax.experimental import pallas as pl
from jax.experimental.pallas import tpu as pltpu

F32 = jnp.float32
BF16 = jnp.bfloat16

EPS = 1e-6
CHUNK = 64
CONV_K = 3
CONV_GROUPS = 8
SGU_GROUPS = 4
SGU_BLOCK = 128
PEER_HEADS = 8
N_KEYS = 128
PEER_TOPK = 16
SQRT_HALF = math.sqrt(0.5)
NEG_BIG = -3.0e38
POS_BIG = 3.0e38

LANES = 128
SUBLANES = 8

MIXER_ROWS = 512
PEER_TOKENS = 512
PEER_KEY_ROWS = 8
MXU_PIECES = 4
GATE_ROWS = 4
GATE_KEYS = 32
MIXER_VMEM_BYTES = 48 * 1024 * 1024
PEER_VMEM_BYTES = 58 * 1024 * 1024

def _oddeven_merge(lo, hi, r):
    step = r * 2
    if step < hi - lo:
        yield from _oddeven_merge(lo, hi, step)
        yield from _oddeven_merge(lo + r, hi, step)
        yield from [(i, i + r) for i in range(lo + r, hi - r, step)]
    else:
        yield (lo, lo + r)


def _oddeven_merge_sort(lo, hi):
    if hi - lo >= 1:
        mid = lo + (hi - lo) // 2
        yield from _oddeven_merge_sort(lo, mid)
        yield from _oddeven_merge_sort(mid + 1, hi)
        yield from _oddeven_merge(lo, hi, 1)


_SORT_NET = tuple(_oddeven_merge_sort(0, PEER_TOPK - 1))
_BITONIC_NET = tuple((i, i | d) for d in (PEER_TOPK >> k for k in range(1, PEER_TOPK.bit_length()))
                     for i in range(PEER_TOPK) if i & d == 0)
_PAIR_ROWS = tuple(PEER_TOPK // (b + 1) for b in range(PEER_TOPK))


def _apply_net(net, xs):
    xs = list(xs)
    for i, j in net:
        a, b = xs[i], xs[j]
        xs[i], xs[j] = jnp.maximum(a, b), jnp.minimum(a, b)
    return xs


def _merge_top(a, b):
    return _apply_net(_BITONIC_NET, [jnp.maximum(a[j], b[PEER_TOPK - 1 - j]) for j in range(PEER_TOPK)])


def _top16_and_next(xs):
    groups = [_apply_net(_SORT_NET, xs[g:g + PEER_TOPK]) for g in range(0, len(xs), PEER_TOPK)]
    while len(groups) > 1:
        groups = [_merge_top(groups[g], groups[g + 1]) for g in range(0, len(groups), 2)]
    top = groups[0]
    nxt = jnp.full(top[0].shape, NEG_BIG, F32)
    for x in xs:
        nxt = jnp.maximum(nxt, jnp.where(x < top[-1], x, NEG_BIG))
    return top, nxt


def _gelu(a):
    return 0.5 * a * (1.0 + lax.erf(a * SQRT_HALF))


def _group_mean_sq(y, ones_ref, group):
    tot = jnp.dot((y * y).astype(BF16), ones_ref[...], preferred_element_type=F32)
    return tot * (1.0 / group)


def _mixer_kernel(x_ref, g_attn_ref, w_in_ref, conv_w_ref, sgu_w_ref, sgu_bias_ref, g_v_ref,
                  g_oc_ref, g_os_ref, ones_c_ref, ones_s_ref, w_out_ref, h_ref,
                  tail_ref, mixed_ref, *, rows, seq_blocks, conv_width, sgu_width):
    i = pl.program_id(0)

    @pl.when(i % seq_blocks == 0)
    def _():
        tail_ref[...] = jnp.zeros_like(tail_ref)

    x = x_ref[...]
    xn = x * lax.rsqrt(jnp.mean(x * x, axis=-1, keepdims=True) + EPS) * g_attn_ref[...]
    proj = jnp.dot(xn.astype(BF16), w_in_ref[...], preferred_element_type=F32)

    c0, c1, c2, c3 = conv_width, 2 * conv_width, 3 * conv_width, 3 * conv_width + sgu_width
    b_gate, c_gate, hc = proj[:, :c0], proj[:, c0:c1], proj[:, c1:c2]

    z = c_gate * hc
    tail = tail_ref[...]
    row = lax.broadcasted_iota(jnp.int32, z.shape, 0)
    z_m1 = jnp.where(row == 0, tail[SUBLANES - 1:SUBLANES, :], pltpu.roll(z, 1, 0))
    z_m2 = jnp.where(row == 0, tail[SUBLANES - 2:SUBLANES - 1, :],
                     jnp.where(row == 1, tail[SUBLANES - 1:SUBLANES, :], pltpu.roll(z, 2, 0)))
    cw = conv_w_ref[...]
    conv = cw[0:1, :] * z_m2 + cw[1:2, :] * z_m1 + cw[2:3, :] * z
    y_conv = b_gate * conv
    tail_ref[...] = z[rows - SUBLANES:, :]

    zz = _gelu(proj[:, c2:])
    u, v = zz[:, :sgu_width], zz[:, sgu_width:]
    gw = sgu_width // SGU_GROUPS
    vn = v * lax.rsqrt(_group_mean_sq(v, ones_s_ref, gw) + EPS) * g_v_ref[...]
    vnb = vn.astype(BF16)
    ri = lax.broadcasted_iota(jnp.int32, (SGU_BLOCK, SGU_BLOCK), 0) // CHUNK
    ci = lax.broadcasted_iota(jnp.int32, (SGU_BLOCK, SGU_BLOCK), 1) // CHUNK
    causal = ri >= ci
    n_pos = rows // SGU_BLOCK
    for g in range(SGU_GROUPS):
        wg = jnp.where(causal, sgu_w_ref[g], 0.0).astype(BF16)
        cols = slice(g * gw, (g + 1) * gw)
        stacked = jnp.concatenate(
            [vnb[pb * SGU_BLOCK:(pb + 1) * SGU_BLOCK, cols] for pb in range(n_pos)], axis=1)
        res = jnp.dot(wg, stacked, preferred_element_type=F32)
        for pb in range(n_pos):
            mixed_ref[pb * SGU_BLOCK:(pb + 1) * SGU_BLOCK, cols] = (
                res[:, pb * gw:(pb + 1) * gw] + sgu_bias_ref[:, cols])
    y_sgu = u * mixed_ref[...]

    yc = y_conv * lax.rsqrt(_group_mean_sq(y_conv, ones_c_ref, conv_width // CONV_GROUPS) + EPS) * g_oc_ref[...]
    ys = y_sgu * lax.rsqrt(_group_mean_sq(y_sgu, ones_s_ref, gw) + EPS) * g_os_ref[...]
    y = jnp.concatenate([yc, ys], axis=1).astype(BF16)
    h_ref[...] = x + jnp.dot(y, w_out_ref[...], preferred_element_type=F32)


def _block_ones(width, group):
    idx = jnp.arange(width) // group
    return (idx[:, None] == idx[None, :]).astype(BF16)


def _mixer(h, seq, g_attn, w_in, conv_w, sgu_w, sgu_b, g_v, g_oc, g_os, w_out):
    n_tok, d_model = h.shape
    conv_width = conv_w.shape[1]
    sgu_width = g_v.shape[0]
    d_in = w_in.shape[1]
    rows = MIXER_ROWS
    assert seq % rows == 0 and rows % SGU_BLOCK == 0 and d_in == 3 * conv_width + 2 * sgu_width
    gw = sgu_width // SGU_GROUPS
    assert gw == SGU_BLOCK
    sgu_bias = jnp.repeat(sgu_b.T, gw, axis=1)
    const = lambda *shape: pl.BlockSpec(shape, lambda i: (0,) * len(shape))
    return pl.pallas_call(
        functools.partial(_mixer_kernel, rows=rows, seq_blocks=seq // rows,
                          conv_width=conv_width, sgu_width=sgu_width),
        grid=(n_tok // rows,),
        in_specs=[
            pl.BlockSpec((rows, d_model), lambda i: (i, 0)),
            const(1, d_model),
            const(d_model, d_in),
            const(CONV_K, conv_width),
            const(SGU_GROUPS, SGU_BLOCK, SGU_BLOCK),
            const(SGU_BLOCK, sgu_width),
            const(1, sgu_width),
            const(1, conv_width),
            const(1, sgu_width),
            const(conv_width, conv_width),
            const(sgu_width, sgu_width),
            const(conv_width + sgu_width, d_model),
        ],
        out_specs=pl.BlockSpec((rows, d_model), lambda i: (i, 0)),
        out_shape=jax.ShapeDtypeStruct((n_tok, d_model), F32),
        scratch_shapes=[pltpu.VMEM((SUBLANES, conv_width), F32),
                        pltpu.VMEM((rows, sgu_width), F32)],
        compiler_params=pltpu.CompilerParams(dimension_semantics=("arbitrary",),
                                             vmem_limit_bytes=MIXER_VMEM_BYTES),
        name="mixer",
    )(h, g_attn.reshape(1, -1), w_in.astype(BF16), conv_w, sgu_w, sgu_bias, g_v.reshape(1, -1),
      g_oc.reshape(1, -1), g_os.reshape(1, -1),
      _block_ones(conv_width, conv_width // CONV_GROUPS), _block_ones(sgu_width, gw),
      w_out.astype(BF16))


def _peer_kernel(h_ref, g_ffn_ref, wq_t_ref, keys_ref, u_ref, v_t_ref, g_fin_ref, out_ref,
                 h_t_ref, hn_t_ref, acc_ref, s2_ref, e2_ref, theta_ref, e1_ref,
                 act0_ref, act1_ref, w0_ref, w1_ref, q_ref, sb0_ref, sb1_ref, top_ref, misc_ref,
                 *, tokens, n_chunks, final_norm):
    s = pl.program_id(1)
    lane_tiles = tokens // LANES
    act_refs = (act0_ref, act1_ref)
    w_refs = (w0_ref, w1_ref)
    chunk_rows = act0_ref.shape[0]
    d_model = acc_ref.shape[0]
    quarter_rows = chunk_rows // MXU_PIECES
    quarter_d = d_model // MXU_PIECES

    def lane_bcast(g_ref):
        g = g_ref[...]
        return jnp.concatenate([g] * lane_tiles, axis=1)

    def route():
        h_t = h_ref[...].T
        h_t_ref[...] = h_t
        ms = jnp.mean(h_t * h_t, axis=0, keepdims=True)
        hn_t_ref[:, 0:tokens] = (h_t * lax.rsqrt(ms + EPS) * lane_bcast(g_ffn_ref)).astype(BF16)
        acc_ref[:, 0:tokens] = jnp.zeros(h_t.shape, F32)

        n_q = PEER_HEADS * N_KEYS
        for p in (1, 0):
            q = jnp.dot(wq_t_ref[p * n_q:(p + 1) * n_q, :], hn_t_ref[:, 0:tokens], preferred_element_type=F32)
            q_ref[p] = q.astype(BF16)
        score_refs = (sb0_ref, sb1_ref)
        for p in (1, 0):
            for hd in range(PEER_HEADS):
                sc = jnp.dot(keys_ref[p * PEER_HEADS + hd], q_ref[p, hd * N_KEYS:(hd + 1) * N_KEYS, :],
                             preferred_element_type=F32)
                if p == 1:
                    s2_ref[hd, :, 0:tokens] = sc
                for lt in range(lane_tiles):
                    score_refs[p][lt, pl.ds(hd, N_KEYS, stride=PEER_HEADS), :] = sc[:, lt * LANES:(lt + 1) * LANES]

        def lanes_of(lt):
            return pl.ds(pl.multiple_of(lt * LANES, LANES), LANES)

        def sort_half(p):
            def body(lt, carry):
                xs = [score_refs[p][lt, k * PEER_HEADS:(k + 1) * PEER_HEADS, :] for k in range(N_KEYS)]
                top, nxt = _top16_and_next(xs)
                ls = lanes_of(lt)
                for j, val in enumerate(top + [nxt]):
                    top_ref[p, j * PEER_HEADS:(j + 1) * PEER_HEADS, ls] = val
                return carry
            return body

        def thresholds(lt, carry):
            act_piece(0, lt)
            ls = lanes_of(lt)
            tops, mids = [], []
            for p in range(2):
                top = [top_ref[p, j * PEER_HEADS:(j + 1) * PEER_HEADS, ls] for j in range(PEER_TOPK + 1)]
                mids.append([0.5 * (top[j] + top[j + 1]) for j in range(PEER_TOPK)])
                tops.append(top[:PEER_TOPK])
            v1, v2 = tops
            pair = {(a, b): v1[a] + v2[b] for b in range(PEER_TOPK) for a in range(_PAIR_ROWS[b])}
            neg = jnp.full(v1[0].shape, NEG_BIG, F32)
            best = [pair[(a, 0)] for a in range(PEER_TOPK)]
            rest = [pair[(a, b)] for b in range(1, PEER_TOPK) for a in range(_PAIR_ROWS[b])]
            rest = rest + [neg] * (-len(rest) % PEER_TOPK)
            for g in range(0, len(rest), PEER_TOPK):
                best = _merge_top(best, _apply_net(_SORT_NET, rest[g:g + PEER_TOPK]))
            t16 = best[-1]
            z = jnp.exp(best[0] - best[0])
            for k in range(1, PEER_TOPK):
                z = z + jnp.exp(best[k] - best[0])
            thr = []
            for a in range(PEER_TOPK):
                cnt = jnp.zeros(t16.shape, F32)
                for b in range(PEER_TOPK // (a + 1)):
                    cnt = cnt + jnp.where(pair[(a, b)] >= t16, 1.0, 0.0)
                th = jnp.full(t16.shape, POS_BIG, F32)
                for j in range(PEER_TOPK // (a + 1)):
                    th = jnp.where(cnt >= float(j + 1), mids[1][j], th)
                thr.append(th)
            misc_ref[0:PEER_HEADS, ls] = v2[0]
            misc_ref[PEER_HEADS:2 * PEER_HEADS, ls] = 1.0 / z
            for k in range(N_KEYS):
                rows = slice(k * PEER_HEADS, (k + 1) * PEER_HEADS)
                x = sb0_ref[lt, rows, :]
                theta = jnp.full(x.shape, POS_BIG, F32)
                for a in reversed(range(PEER_TOPK)):
                    theta = jnp.where(x >= mids[0][a], thr[a], theta)
                theta_ref[lt, rows, :] = theta
                e1_ref[lt, rows, :] = SQRT_HALF * jnp.exp(x - v1[0])
            return carry

        lax.fori_loop(0, lane_tiles, sort_half(1), 0)
        lax.fori_loop(0, lane_tiles, sort_half(0), 0)
        lax.fori_loop(0, lane_tiles, thresholds, 0)

        for hd in range(PEER_HEADS):
            m2 = misc_ref[hd:hd + 1, :]
            z_inv = misc_ref[PEER_HEADS + hd:PEER_HEADS + hd + 1, :]
            e2_ref[hd, :, 0:tokens] = jnp.exp(s2_ref[hd, :, 0:tokens] - m2) * z_inv

    def act_piece(slot, mq):
        rows = pl.ds(pl.multiple_of(mq * quarter_rows, quarter_rows), quarter_rows)
        act_refs[slot][rows, 0:tokens] = jnp.dot(u_ref[rows, 0:d_model], hn_t_ref[:, 0:tokens],
                                                 preferred_element_type=F32)

    def out_piece(slot, mq):
        rows = slice(mq * quarter_d, (mq + 1) * quarter_d)
        acc_ref[rows, 0:tokens] += jnp.dot(v_t_ref[rows, 0:chunk_rows], w_refs[slot][:, 0:tokens],
                                           preferred_element_type=F32)

    def w_block(key_base, slot, r0, lt, q0):
        ls = slice(lt * LANES, (lt + 1) * LANES)
        qs = slice(q0, q0 + GATE_KEYS)
        gates = [jnp.zeros((GATE_KEYS, LANES), F32) for _ in range(GATE_ROWS)]
        for hd in range(PEER_HEADS):
            s2 = s2_ref[hd, qs, ls]
            e2 = e2_ref[hd, qs, ls]
            for k in range(GATE_ROWS):
                row = pl.ds(key_base + (r0 + k) * PEER_HEADS + hd, 1)
                theta = jnp.broadcast_to(theta_ref[lt, row, :], (GATE_KEYS, LANES))
                e1 = jnp.broadcast_to(e1_ref[lt, row, :], (GATE_KEYS, LANES))
                gates[k] = gates[k] + jnp.where(s2 >= theta, e2, 0.0) * e1
        for k in range(GATE_ROWS):
            rows = slice((r0 + k) * N_KEYS + q0, (r0 + k) * N_KEYS + q0 + GATE_KEYS)
            x = act_refs[slot][rows, ls]
            w_refs[slot][rows, ls] = (gates[k] * (x * (1.0 + lax.erf(x)))).astype(BF16)

    def step(act_slot=None, build=None, out_slot=None):
        mxu = []
        for mq in range(MXU_PIECES):
            if act_slot is not None:
                mxu.append(functools.partial(act_piece, act_slot, mq))
            if out_slot is not None:
                mxu.append(functools.partial(out_piece, out_slot, mq))
        tiles = []
        if build is not None:
            chunk, slot = build
            key_base = pl.multiple_of(chunk * (PEER_KEY_ROWS * PEER_HEADS), PEER_KEY_ROWS * PEER_HEADS)
            tiles = [functools.partial(w_block, key_base, slot, r0, lt, q0)
                     for lt in range(lane_tiles)
                     for r0 in range(0, PEER_KEY_ROWS, GATE_ROWS)
                     for q0 in range(0, N_KEYS, GATE_KEYS)]
        n_groups = max(len(mxu), 1)
        per_group = -(-len(tiles) // n_groups)
        lead = per_group // 2 if mxu else 0
        for tile in tiles[:lead]:
            tile()
        for g in range(n_groups):
            if g < len(mxu):
                mxu[g]()
            for tile in tiles[lead + g * per_group:lead + (g + 1) * per_group]:
                tile()

    @pl.when(s == 0)
    def _():
        route()

    @pl.when(s == 1)
    def _():
        step(act_slot=1, build=(0, 0))

    for parity in (0, 1):
        @pl.when(jnp.logical_and(jnp.logical_and(s >= 2, s < n_chunks), s % 2 == parity))
        def _():
            step(act_slot=parity, build=(s - 1, 1 - parity), out_slot=parity)

    @pl.when(s == n_chunks)
    def _():
        step(build=(n_chunks - 1, (n_chunks - 1) % 2), out_slot=n_chunks % 2)

    @pl.when(s == n_chunks + 1)
    def _():
        step(out_slot=(n_chunks + 1) % 2)
        res = h_t_ref[...] + acc_ref[:, 0:tokens]
        if final_norm:
            ms = jnp.mean(res * res, axis=0, keepdims=True)
            res = res * lax.rsqrt(ms + EPS) * lane_bcast(g_fin_ref)
        out_ref[...] = res.T


def _expert_operands_kernel(u_ref, v_ref, u_out_ref, v_t_out_ref):
    chunk, d_model = u_ref.shape
    u_out_ref[:, 0:d_model] = (SQRT_HALF * u_ref[...]).astype(BF16)
    u_out_ref[:, d_model:] = jnp.zeros((chunk, LANES), BF16)
    v_t_out_ref[:, 0:chunk] = v_ref[...].T.astype(BF16)
    v_t_out_ref[:, chunk:] = jnp.zeros((d_model, LANES), BF16)


def _expert_operands(expert_u, expert_v, chunk):
    n_experts, d_model = expert_u.shape
    n_chunks = n_experts // chunk
    return pl.pallas_call(
        _expert_operands_kernel,
        grid=(n_chunks,),
        in_specs=[pl.BlockSpec((chunk, d_model), lambda c: (c, 0)),
                  pl.BlockSpec((chunk, d_model), lambda c: (c, 0))],
        out_specs=[pl.BlockSpec((chunk, d_model + LANES), lambda c: (c, 0)),
                   pl.BlockSpec((d_model, chunk + LANES), lambda c: (0, c))],
        out_shape=[jax.ShapeDtypeStruct((n_experts, d_model + LANES), BF16),
                   jax.ShapeDtypeStruct((d_model, n_chunks * (chunk + LANES)), BF16)],
        compiler_params=pltpu.CompilerParams(dimension_semantics=("arbitrary",),
                                             vmem_limit_bytes=MIXER_VMEM_BYTES),
        name="expert_operands",
    )(expert_u, expert_v)


def _query_operand_kernel(w_ref, w_t_ref):
    w_t_ref[...] = w_ref[...].T.astype(BF16)


def _query_operand(w_q, d_half):
    d_model, n_cols = w_q.shape
    n_blocks = n_cols // d_half
    return pl.pallas_call(
        _query_operand_kernel,
        grid=(n_blocks,),
        in_specs=[pl.BlockSpec((d_model, d_half), lambda c: (0, c))],
        out_specs=pl.BlockSpec((d_half, d_model), lambda c: ((c % 2) * (n_blocks // 2) + c // 2, 0)),
        out_shape=jax.ShapeDtypeStruct((n_cols, d_model), BF16),
        compiler_params=pltpu.CompilerParams(dimension_semantics=("arbitrary",)),
        name="query_operand",
    )(w_q)


def _peer(h, g_ffn, w_q, sub_keys, expert_u, expert_v, g_final):
    n_tok, d_model = h.shape
    n_experts = expert_u.shape[0]
    d_half = sub_keys.shape[-1]
    assert sub_keys.shape[:3] == (PEER_HEADS, 2, N_KEYS) and d_half == N_KEYS
    assert n_experts == N_KEYS * N_KEYS and w_q.shape == (d_model, PEER_HEADS * 2 * d_half)
    tokens = PEER_TOKENS
    chunk = PEER_KEY_ROWS * N_KEYS
    n_chunks = n_experts // chunk
    assert n_tok % tokens == 0 and n_chunks >= 3 and tokens // LANES == MXU_PIECES
    final_norm = g_final is not None
    wq_t = _query_operand(w_q, d_half)
    keys = sub_keys.transpose(1, 0, 2, 3).reshape(2 * PEER_HEADS, N_KEYS, d_half).astype(BF16)
    g_fin = g_final if final_norm else jnp.ones((d_model,), F32)
    lanes = lambda g: jnp.broadcast_to(g.astype(F32)[:, None], (d_model, LANES))
    const = lambda *shape, **kw: pl.BlockSpec(shape, lambda i, s: (0,) * len(shape), **kw)
    padded = tokens + LANES
    u_pad, v_t_pad = _expert_operands(expert_u, expert_v, chunk)
    table = lambda dtype: pltpu.VMEM((PEER_HEADS, N_KEYS, padded), dtype)
    return pl.pallas_call(
        functools.partial(_peer_kernel, tokens=tokens, n_chunks=n_chunks, final_norm=final_norm),
        grid=(n_tok // tokens, n_chunks + 2),
        in_specs=[
            pl.BlockSpec((tokens, d_model), lambda i, s: (i, 0), pipeline_mode=pl.Buffered(1)),
            const(d_model, LANES),
            const(PEER_HEADS * 2 * d_half, d_model, pipeline_mode=pl.Buffered(1)),
            const(2 * PEER_HEADS, N_KEYS, d_half),
            pl.BlockSpec((chunk, d_model + LANES), lambda i, s: (jnp.minimum(s, n_chunks - 1), 0)),
            pl.BlockSpec((d_model, chunk + LANES), lambda i, s: (0, jnp.clip(s - 2, 0, n_chunks - 1))),
            const(d_model, LANES),
        ],
        out_specs=pl.BlockSpec((tokens, d_model), lambda i, s: (i, 0)),
        out_shape=jax.ShapeDtypeStruct((n_tok, d_model), F32),
        scratch_shapes=[
            pltpu.VMEM((d_model, tokens), F32),
            pltpu.VMEM((d_model, padded), BF16),
            pltpu.VMEM((d_model, padded), F32),
            table(F32),
            table(F32),
            pltpu.VMEM((tokens // LANES, N_KEYS * PEER_HEADS, LANES), F32),
            pltpu.VMEM((tokens // LANES, N_KEYS * PEER_HEADS, LANES), F32),
            pltpu.VMEM((chunk, padded), F32),
            pltpu.VMEM((chunk, padded), F32),
            pltpu.VMEM((chunk, padded), BF16),
            pltpu.VMEM((chunk, padded), BF16),
            pltpu.VMEM((2, PEER_HEADS * N_KEYS, tokens), BF16),
            pltpu.VMEM((tokens // LANES, PEER_HEADS * N_KEYS, LANES), F32),
            pltpu.VMEM((tokens // LANES, PEER_HEADS * N_KEYS, LANES), F32),
            pltpu.VMEM((2, (PEER_TOPK + 1) * PEER_HEADS, tokens), F32),
            pltpu.VMEM((2 * PEER_HEADS, tokens), F32),
        ],
        compiler_params=pltpu.CompilerParams(dimension_semantics=("arbitrary", "arbitrary"),
                                             vmem_limit_bytes=PEER_VMEM_BYTES),
        name="peer",
    )(h, lanes(g_ffn), wq_t, keys,
      u_pad, v_t_pad, lanes(g_fin))


def kernel(x, attn_norm_g, w_in, conv_w, sgu_w, sgu_b, sgu_norm_g, out_norm_conv_g, out_norm_sgu_g,
           w_out, ffn_norm_g, peer_w_q, peer_sub_keys, peer_u, peer_v, final_norm_g):
    batch, seq, d_model = x.shape
    depth = attn_norm_g.shape[0]
    h = x.reshape(batch * seq, d_model)
    for l in range(depth):
        h = _mixer(h, seq, attn_norm_g[l], w_in[l], conv_w[l], sgu_w[l], sgu_b[l], sgu_norm_g[l],
                   out_norm_conv_g[l], out_norm_sgu_g[l], w_out[l])
        h = _peer(h, ffn_norm_g[l], peer_w_q[l], peer_sub_keys[l], peer_u[l], peer_v[l],
                  final_norm_g if l == depth - 1 else None)
    return h.reshape(batch, seq, d_model)
```

```python
import functools
import math

import jax
import jax.numpy as jnp
from jax import lax
from jax.experimental import pallas as pl
from jax.experimental.pallas import tpu as pltpu

F32 = jnp.float32
BF16 = jnp.bfloat16

EPS = 1e-6
CHUNK = 64
CONV_K = 3
CONV_GROUPS = 8
SGU_GROUPS = 4
SGU_BLOCK = 128
PEER_HEADS = 8
N_KEYS = 128
PEER_TOPK = 16
SQRT_HALF = math.sqrt(0.5)
NEG_BIG = -3.0e38
POS_BIG = 3.0e38

LANES = 128
SUBLANES = 8

MIXER_ROWS = 512
PEER_TOKENS = 512
PEER_KEY_ROWS = 8
MXU_PIECES = 4
GATE_ROWS = 4
GATE_KEYS = 32
MIXER_VMEM_BYTES = 48 * 1024 * 1024
PEER_VMEM_BYTES = 58 * 1024 * 1024

def _oddeven_merge(lo, hi, r):
    step = r * 2
    if step < hi - lo:
        yield from _oddeven_merge(lo, hi, step)
        yield from _oddeven_merge(lo + r, hi, step)
        yield from [(i, i + r) for i in range(lo + r, hi - r, step)]
    else:
        yield (lo, lo + r)


def _oddeven_merge_sort(lo, hi):
    if hi - lo >= 1:
        mid = lo + (hi - lo) // 2
        yield from _oddeven_merge_sort(lo, mid)
        yield from _oddeven_merge_sort(mid + 1, hi)
        yield from _oddeven_merge(lo, hi, 1)


_SORT_NET = tuple(_oddeven_merge_sort(0, PEER_TOPK - 1))
_BITONIC_NET = tuple((i, i | d) for d in (PEER_TOPK >> k for k in range(1, PEER_TOPK.bit_length()))
                     for i in range(PEER_TOPK) if i & d == 0)
_PAIR_ROWS = tuple(PEER_TOPK // (b + 1) for b in range(PEER_TOPK))


def _apply_net(net, xs):
    xs = list(xs)
    for i, j in net:
        a, b = xs[i], xs[j]
        xs[i], xs[j] = jnp.maximum(a, b), jnp.minimum(a, b)
    return xs


def _merge_top(a, b):
    return _apply_net(_BITONIC_NET, [jnp.maximum(a[j], b[PEER_TOPK - 1 - j]) for j in range(PEER_TOPK)])


def _top16_and_next(xs):
    groups = [_apply_net(_SORT_NET, xs[g:g + PEER_TOPK]) for g in range(0, len(xs), PEER_TOPK)]
    while len(groups) > 1:
        groups = [_merge_top(groups[g], groups[g + 1]) for g in range(0, len(groups), 2)]
    top = groups[0]
    nxt = jnp.full(top[0].shape, NEG_BIG, F32)
    for x in xs:
        nxt = jnp.maximum(nxt, jnp.where(x < top[-1], x, NEG_BIG))
    return top, nxt


def _gelu(a):
    return 0.5 * a * (1.0 + lax.erf(a * SQRT_HALF))


def _group_mean_sq(y, ones_ref, group):
    tot = jnp.dot((y * y).astype(BF16), ones_ref[...], preferred_element_type=F32)
    return tot * (1.0 / group)


def _mixer_kernel(x_ref, g_attn_ref, w_in_ref, conv_w_ref, sgu_w_ref, sgu_bias_ref, g_v_ref,
                  g_oc_ref, g_os_ref, ones_c_ref, ones_s_ref, w_out_ref, h_t_ref,
                  tail_ref, mixed_ref, *, rows, seq_blocks, conv_width, sgu_width):
    i = pl.program_id(0)

    @pl.when(i % seq_blocks == 0)
    def _():
        tail_ref[...] = jnp.zeros_like(tail_ref)

    x = x_ref[...]
    xn = x * lax.rsqrt(jnp.mean(x * x, axis=-1, keepdims=True) + EPS) * g_attn_ref[...]
    proj = jnp.dot(xn.astype(BF16), w_in_ref[...], preferred_element_type=F32)

    c0, c1, c2, c3 = conv_width, 2 * conv_width, 3 * conv_width, 3 * conv_width + sgu_width
    b_gate, c_gate, hc = proj[:, :c0], proj[:, c0:c1], proj[:, c1:c2]

    z = c_gate * hc
    tail = tail_ref[...]
    row = lax.broadcasted_iota(jnp.int32, z.shape, 0)
    z_m1 = jnp.where(row == 0, tail[SUBLANES - 1:SUBLANES, :], pltpu.roll(z, 1, 0))
    z_m2 = jnp.where(row == 0, tail[SUBLANES - 2:SUBLANES - 1, :],
                     jnp.where(row == 1, tail[SUBLANES - 1:SUBLANES, :], pltpu.roll(z, 2, 0)))
    cw = conv_w_ref[...]
    conv = cw[0:1, :] * z_m2 + cw[1:2, :] * z_m1 + cw[2:3, :] * z
    y_conv = b_gate * conv
    tail_ref[...] = z[rows - SUBLANES:, :]

    zz = _gelu(proj[:, c2:])
    u, v = zz[:, :sgu_width], zz[:, sgu_width:]
    gw = sgu_width // SGU_GROUPS
    vn = v * lax.rsqrt(_group_mean_sq(v, ones_s_ref, gw) + EPS) * g_v_ref[...]
    vnb = vn.astype(BF16)
    ri = lax.broadcasted_iota(jnp.int32, (SGU_BLOCK, SGU_BLOCK), 0) // CHUNK
    ci = lax.broadcasted_iota(jnp.int32, (SGU_BLOCK, SGU_BLOCK), 1) // CHUNK
    causal = ri >= ci
    n_pos = rows // SGU_BLOCK
    for g in range(SGU_GROUPS):
        wg = jnp.where(causal, sgu_w_ref[g], 0.0).astype(BF16)
        cols = slice(g * gw, (g + 1) * gw)
        stacked = jnp.concatenate(
            [vnb[pb * SGU_BLOCK:(pb + 1) * SGU_BLOCK, cols] for pb in range(n_pos)], axis=1)
        res = jnp.dot(wg, stacked, preferred_element_type=F32)
        for pb in range(n_pos):
            mixed_ref[pb * SGU_BLOCK:(pb + 1) * SGU_BLOCK, cols] = (
                res[:, pb * gw:(pb + 1) * gw] + sgu_bias_ref[:, cols])
    y_sgu = u * mixed_ref[...]

    yc = y_conv * lax.rsqrt(_group_mean_sq(y_conv, ones_c_ref, conv_width // CONV_GROUPS) + EPS) * g_oc_ref[...]
    ys = y_sgu * lax.rsqrt(_group_mean_sq(y_sgu, ones_s_ref, gw) + EPS) * g_os_ref[...]
    y = jnp.concatenate([yc, ys], axis=1).astype(BF16)
    h_t_ref[...] = (x + jnp.dot(y, w_out_ref[...], preferred_element_type=F32)).T


def _block_ones(width, group):
    idx = jnp.arange(width) // group
    return (idx[:, None] == idx[None, :]).astype(BF16)


def _mixer(h, seq, g_attn, w_in, conv_w, sgu_w, sgu_b, g_v, g_oc, g_os, w_out):
    n_tok, d_model = h.shape
    conv_width = conv_w.shape[1]
    sgu_width = g_v.shape[0]
    d_in = w_in.shape[1]
    rows = MIXER_ROWS
    assert seq % rows == 0 and rows % SGU_BLOCK == 0 and d_in == 3 * conv_width + 2 * sgu_width
    gw = sgu_width // SGU_GROUPS
    assert gw == SGU_BLOCK
    sgu_bias = jnp.repeat(sgu_b.T, gw, axis=1)
    const = lambda *shape: pl.BlockSpec(shape, lambda i: (0,) * len(shape))
    return pl.pallas_call(
        functools.partial(_mixer_kernel, rows=rows, seq_blocks=seq // rows,
                          conv_width=conv_width, sgu_width=sgu_width),
        grid=(n_tok // rows,),
        in_specs=[
            pl.BlockSpec((rows, d_model), lambda i: (i, 0)),
            const(1, d_model),
            const(d_model, d_in),
            const(CONV_K, conv_width),
            const(SGU_GROUPS, SGU_BLOCK, SGU_BLOCK),
            const(SGU_BLOCK, sgu_width),
            const(1, sgu_width),
            const(1, conv_width),
            const(1, sgu_width),
            const(conv_width, conv_width),
            const(sgu_width, sgu_width),
            const(conv_width + sgu_width, d_model),
        ],
        out_specs=pl.BlockSpec((d_model, rows), lambda i: (0, i)),
        out_shape=jax.ShapeDtypeStruct((d_model, n_tok), F32),
        scratch_shapes=[pltpu.VMEM((SUBLANES, conv_width), F32),
                        pltpu.VMEM((rows, sgu_width), F32)],
        compiler_params=pltpu.CompilerParams(dimension_semantics=("arbitrary",),
                                             vmem_limit_bytes=MIXER_VMEM_BYTES),
        name="mixer",
    )(h, g_attn.reshape(1, -1), w_in.astype(BF16), conv_w, sgu_w, sgu_bias, g_v.reshape(1, -1),
      g_oc.reshape(1, -1), g_os.reshape(1, -1),
      _block_ones(conv_width, conv_width // CONV_GROUPS), _block_ones(sgu_width, gw),
      w_out.astype(BF16))


def _peer_kernel(h_ref, g_ffn_ref, wq_t_ref, keys_ref, u_ref, v_t_ref, g_fin_ref, out_ref,
                 h_t_ref, hn_t_ref, acc_ref, s2_ref, e2_ref, theta_ref, e1_ref,
                 act0_ref, act1_ref, w0_ref, w1_ref, q_ref, sb0_ref, sb1_ref, top_ref, misc_ref,
                 *, tokens, n_chunks, final_norm):
    s = pl.program_id(1)
    lane_tiles = tokens // LANES
    act_refs = (act0_ref, act1_ref)
    w_refs = (w0_ref, w1_ref)
    chunk_rows = act0_ref.shape[0]
    d_model = acc_ref.shape[0]
    quarter_rows = chunk_rows // MXU_PIECES
    quarter_d = d_model // MXU_PIECES

    def lane_bcast(g_ref):
        g = g_ref[...]
        return jnp.concatenate([g] * lane_tiles, axis=1)

    def route():
        h_t = h_ref[...]
        h_t_ref[...] = h_t
        ms = jnp.mean(h_t * h_t, axis=0, keepdims=True)
        hn_t_ref[:, 0:tokens] = (h_t * lax.rsqrt(ms + EPS) * lane_bcast(g_ffn_ref)).astype(BF16)
        acc_ref[:, 0:tokens] = jnp.zeros(h_t.shape, F32)

        n_q = PEER_HEADS * N_KEYS
        for p in (1, 0):
            q = jnp.dot(wq_t_ref[p * n_q:(p + 1) * n_q, :], hn_t_ref[:, 0:tokens], preferred_element_type=F32)
            q_ref[p] = q.astype(BF16)
        score_refs = (sb0_ref, sb1_ref)
        for p in (1, 0):
            for hd in range(PEER_HEADS):
                sc = jnp.dot(keys_ref[p * PEER_HEADS + hd], q_ref[p, hd * N_KEYS:(hd + 1) * N_KEYS, :],
                             preferred_element_type=F32)
                if p == 1:
                    s2_ref[hd, :, 0:tokens] = sc
                for lt in range(lane_tiles):
                    score_refs[p][lt, pl.ds(hd, N_KEYS, stride=PEER_HEADS), :] = sc[:, lt * LANES:(lt + 1) * LANES]

        def lanes_of(lt):
            return pl.ds(pl.multiple_of(lt * LANES, LANES), LANES)

        def sort_half(p):
            def body(lt, carry):
                xs = [score_refs[p][lt, k * PEER_HEADS:(k + 1) * PEER_HEADS, :] for k in range(N_KEYS)]
                top, nxt = _top16_and_next(xs)
                ls = lanes_of(lt)
                for j, val in enumerate(top + [nxt]):
                    top_ref[p, j * PEER_HEADS:(j + 1) * PEER_HEADS, ls] = val
                return carry
            return body

        def thresholds(lt, carry):
            act_piece(0, lt)
            ls = lanes_of(lt)
            tops, mids = [], []
            for p in range(2):
                top = [top_ref[p, j * PEER_HEADS:(j + 1) * PEER_HEADS, ls] for j in range(PEER_TOPK + 1)]
                mids.append([0.5 * (top[j] + top[j + 1]) for j in range(PEER_TOPK)])
                tops.append(top[:PEER_TOPK])
            v1, v2 = tops
            pair = {(a, b): v1[a] + v2[b] for b in range(PEER_TOPK) for a in range(_PAIR_ROWS[b])}
            neg = jnp.full(v1[0].shape, NEG_BIG, F32)
            best = [pair[(a, 0)] for a in range(PEER_TOPK)]
            rest = [pair[(a, b)] for b in range(1, PEER_TOPK) for a in range(_PAIR_ROWS[b])]
            rest = rest + [neg] * (-len(rest) % PEER_TOPK)
            for g in range(0, len(rest), PEER_TOPK):
                best = _merge_top(best, _apply_net(_SORT_NET, rest[g:g + PEER_TOPK]))
            t16 = best[-1]
            z = jnp.exp(best[0] - best[0])
            for k in range(1, PEER_TOPK):
                z = z + jnp.exp(best[k] - best[0])
            thr = []
            for a in range(PEER_TOPK):
                cnt = jnp.zeros(t16.shape, F32)
                for b in range(PEER_TOPK // (a + 1)):
                    cnt = cnt + jnp.where(pair[(a, b)] >= t16, 1.0, 0.0)
                th = jnp.full(t16.shape, POS_BIG, F32)
                for j in range(PEER_TOPK // (a + 1)):
                    th = jnp.where(cnt >= float(j + 1), mids[1][j], th)
                thr.append(th)
            misc_ref[0:PEER_HEADS, ls] = v2[0]
            misc_ref[PEER_HEADS:2 * PEER_HEADS, ls] = 1.0 / z
            for k in range(N_KEYS):
                rows = slice(k * PEER_HEADS, (k + 1) * PEER_HEADS)
                x = sb0_ref[lt, rows, :]
                theta = jnp.full(x.shape, POS_BIG, F32)
                for a in reversed(range(PEER_TOPK)):
                    theta = jnp.where(x >= mids[0][a], thr[a], theta)
                theta_ref[lt, rows, :] = theta
                e1_ref[lt, rows, :] = SQRT_HALF * jnp.exp(x - v1[0])
            return carry

        lax.fori_loop(0, lane_tiles, sort_half(1), 0)
        lax.fori_loop(0, lane_tiles, sort_half(0), 0)
        lax.fori_loop(0, lane_tiles, thresholds, 0)

        for hd in range(PEER_HEADS):
            m2 = misc_ref[hd:hd + 1, :]
            z_inv = misc_ref[PEER_HEADS + hd:PEER_HEADS + hd + 1, :]
            e2_ref[hd, :, 0:tokens] = jnp.exp(s2_ref[hd, :, 0:tokens] - m2) * z_inv

    def act_piece(slot, mq):
        rows = pl.ds(pl.multiple_of(mq * quarter_rows, quarter_rows), quarter_rows)
        act_refs[slot][rows, 0:tokens] = jnp.dot(u_ref[rows, 0:d_model], hn_t_ref[:, 0:tokens],
                                                 preferred_element_type=F32)

    def out_piece(slot, mq):
        rows = slice(mq * quarter_d, (mq + 1) * quarter_d)
        acc_ref[rows, 0:tokens] += jnp.dot(v_t_ref[rows, 0:chunk_rows], w_refs[slot][:, 0:tokens],
                                           preferred_element_type=F32)

    def w_block(key_base, slot, r0, lt, q0):
        ls = slice(lt * LANES, (lt + 1) * LANES)
        qs = slice(q0, q0 + GATE_KEYS)
        gates = [jnp.zeros((GATE_KEYS, LANES), F32) for _ in range(GATE_ROWS)]
        for hd in range(PEER_HEADS):
            s2 = s2_ref[hd, qs, ls]
            e2 = e2_ref[hd, qs, ls]
            for k in range(GATE_ROWS):
                row = pl.ds(key_base + (r0 + k) * PEER_HEADS + hd, 1)
                theta = jnp.broadcast_to(theta_ref[lt, row, :], (GATE_KEYS, LANES))
                e1 = jnp.broadcast_to(e1_ref[lt, row, :], (GATE_KEYS, LANES))
                gates[k] = gates[k] + jnp.where(s2 >= theta, e2, 0.0) * e1
        for k in range(GATE_ROWS):
            rows = slice((r0 + k) * N_KEYS + q0, (r0 + k) * N_KEYS + q0 + GATE_KEYS)
            x = act_refs[slot][rows, ls]
            w_refs[slot][rows, ls] = (gates[k] * (x * (1.0 + lax.erf(x)))).astype(BF16)

    def step(act_slot=None, build=None, out_slot=None):
        mxu = []
        for mq in range(MXU_PIECES):
            if act_slot is not None:
                mxu.append(functools.partial(act_piece, act_slot, mq))
            if out_slot is not None:
                mxu.append(functools.partial(out_piece, out_slot, mq))
        tiles = []
        if build is not None:
            chunk, slot = build
            key_base = pl.multiple_of(chunk * (PEER_KEY_ROWS * PEER_HEADS), PEER_KEY_ROWS * PEER_HEADS)
            tiles = [functools.partial(w_block, key_base, slot, r0, lt, q0)
                     for lt in range(lane_tiles)
                     for r0 in range(0, PEER_KEY_ROWS, GATE_ROWS)
                     for q0 in range(0, N_KEYS, GATE_KEYS)]
        n_groups = max(len(mxu), 1)
        per_group = -(-len(tiles) // n_groups)
        lead = per_group // 2 if mxu else 0
        for tile in tiles[:lead]:
            tile()
        for g in range(n_groups):
            if g < len(mxu):
                mxu[g]()
            for tile in tiles[lead + g * per_group:lead + (g + 1) * per_group]:
                tile()

    @pl.when(s == 0)
    def _():
        route()

    @pl.when(s == 1)
    def _():
        step(act_slot=1, build=(0, 0))

    for parity in (0, 1):
        @pl.when(jnp.logical_and(jnp.logical_and(s >= 2, s < n_chunks), s % 2 == parity))
        def _():
            step(act_slot=parity, build=(s - 1, 1 - parity), out_slot=parity)

    @pl.when(s == n_chunks)
    def _():
        step(build=(n_chunks - 1, (n_chunks - 1) % 2), out_slot=n_chunks % 2)

    @pl.when(s == n_chunks + 1)
    def _():
        step(out_slot=(n_chunks + 1) % 2)
        res = h_t_ref[...] + acc_ref[:, 0:tokens]
        if final_norm:
            ms = jnp.mean(res * res, axis=0, keepdims=True)
            res = res * lax.rsqrt(ms + EPS) * lane_bcast(g_fin_ref)
        out_ref[...] = res.T


def _expert_operands_kernel(u_ref, v_ref, u_out_ref, v_t_out_ref):
    chunk, d_model = u_ref.shape
    u_out_ref[:, 0:d_model] = (SQRT_HALF * u_ref[...]).astype(BF16)
    u_out_ref[:, d_model:] = jnp.zeros((chunk, LANES), BF16)
    v_t_out_ref[:, 0:chunk] = v_ref[...].T.astype(BF16)
    v_t_out_ref[:, chunk:] = jnp.zeros((d_model, LANES), BF16)


def _expert_operands(expert_u, expert_v, chunk):
    n_experts, d_model = expert_u.shape
    n_chunks = n_experts // chunk
    return pl.pallas_call(
        _expert_operands_kernel,
        grid=(n_chunks,),
        in_specs=[pl.BlockSpec((chunk, d_model), lambda c: (c, 0)),
                  pl.BlockSpec((chunk, d_model), lambda c: (c, 0))],
        out_specs=[pl.BlockSpec((chunk, d_model + LANES), lambda c: (c, 0)),
                   pl.BlockSpec((d_model, chunk + LANES), lambda c: (0, c))],
        out_shape=[jax.ShapeDtypeStruct((n_experts, d_model + LANES), BF16),
                   jax.ShapeDtypeStruct((d_model, n_chunks * (chunk + LANES)), BF16)],
        compiler_params=pltpu.CompilerParams(dimension_semantics=("arbitrary",),
                                             vmem_limit_bytes=MIXER_VMEM_BYTES),
        name="expert_operands",
    )(expert_u, expert_v)


def _query_operand_kernel(w_ref, w_t_ref):
    w_t_ref[...] = w_ref[...].T.astype(BF16)


def _query_operand(w_q, d_half):
    d_model, n_cols = w_q.shape
    n_blocks = n_cols // d_half
    return pl.pallas_call(
        _query_operand_kernel,
        grid=(n_blocks,),
        in_specs=[pl.BlockSpec((d_model, d_half), lambda c: (0, c))],
        out_specs=pl.BlockSpec((d_half, d_model), lambda c: ((c % 2) * (n_blocks // 2) + c // 2, 0)),
        out_shape=jax.ShapeDtypeStruct((n_cols, d_model), BF16),
        compiler_params=pltpu.CompilerParams(dimension_semantics=("arbitrary",)),
        name="query_operand",
    )(w_q)


def _peer(h_t, g_ffn, w_q, sub_keys, expert_u, expert_v, g_final):
    d_model, n_tok = h_t.shape
    n_experts = expert_u.shape[0]
    d_half = sub_keys.shape[-1]
    assert sub_keys.shape[:3] == (PEER_HEADS, 2, N_KEYS) and d_half == N_KEYS
    assert n_experts == N_KEYS * N_KEYS and w_q.shape == (d_model, PEER_HEADS * 2 * d_half)
    tokens = PEER_TOKENS
    chunk = PEER_KEY_ROWS * N_KEYS
    n_chunks = n_experts // chunk
    assert n_tok % tokens == 0 and n_chunks >= 3 and tokens // LANES == MXU_PIECES
    final_norm = g_final is not None
    wq_t = _query_operand(w_q, d_half)
    keys = sub_keys.transpose(1, 0, 2, 3).reshape(2 * PEER_HEADS, N_KEYS, d_half).astype(BF16)
    g_fin = g_final if final_norm else jnp.ones((d_model,), F32)
    lanes = lambda g: jnp.broadcast_to(g.astype(F32)[:, None], (d_model, LANES))
    const = lambda *shape, **kw: pl.BlockSpec(shape, lambda i, s: (0,) * len(shape), **kw)
    padded = tokens + LANES
    u_pad, v_t_pad = _expert_operands(expert_u, expert_v, chunk)
    table = lambda dtype: pltpu.VMEM((PEER_HEADS, N_KEYS, padded), dtype)
    return pl.pallas_call(
        functools.partial(_peer_kernel, tokens=tokens, n_chunks=n_chunks, final_norm=final_norm),
        grid=(n_tok // tokens, n_chunks + 2),
        in_specs=[
            pl.BlockSpec((d_model, tokens), lambda i, s: (0, i), pipeline_mode=pl.Buffered(1)),
            const(d_model, LANES),
            const(PEER_HEADS * 2 * d_half, d_model, pipeline_mode=pl.Buffered(1)),
            const(2 * PEER_HEADS, N_KEYS, d_half),
            pl.BlockSpec((chunk, d_model + LANES), lambda i, s: (jnp.minimum(s, n_chunks - 1), 0)),
            pl.BlockSpec((d_model, chunk + LANES), lambda i, s: (0, jnp.clip(s - 2, 0, n_chunks - 1))),
            const(d_model, LANES),
        ],
        out_specs=pl.BlockSpec((tokens, d_model), lambda i, s: (i, 0)),
        out_shape=jax.ShapeDtypeStruct((n_tok, d_model), F32),
        scratch_shapes=[
            pltpu.VMEM((d_model, tokens), F32),
            pltpu.VMEM((d_model, padded), BF16),
            pltpu.VMEM((d_model, padded), F32),
            table(F32),
            table(F32),
            pltpu.VMEM((tokens // LANES, N_KEYS * PEER_HEADS, LANES), F32),
            pltpu.VMEM((tokens // LANES, N_KEYS * PEER_HEADS, LANES), F32),
            pltpu.VMEM((chunk, padded), F32),
            pltpu.VMEM((chunk, padded), F32),
            pltpu.VMEM((chunk, padded), BF16),
            pltpu.VMEM((chunk, padded), BF16),
            pltpu.VMEM((2, PEER_HEADS * N_KEYS, tokens), BF16),
            pltpu.VMEM((tokens // LANES, PEER_HEADS * N_KEYS, LANES), F32),
            pltpu.VMEM((tokens // LANES, PEER_HEADS * N_KEYS, LANES), F32),
            pltpu.VMEM((2, (PEER_TOPK + 1) * PEER_HEADS, tokens), F32),
            pltpu.VMEM((2 * PEER_HEADS, tokens), F32),
        ],
        compiler_params=pltpu.CompilerParams(dimension_semantics=("arbitrary", "arbitrary"),
                                             vmem_limit_bytes=PEER_VMEM_BYTES),
        name="peer",
    )(h_t, lanes(g_ffn), wq_t, keys,
      u_pad, v_t_pad, lanes(g_fin))


def kernel(x, attn_norm_g, w_in, conv_w, sgu_w, sgu_b, sgu_norm_g, out_norm_conv_g, out_norm_sgu_g,
           w_out, ffn_norm_g, peer_w_q, peer_sub_keys, peer_u, peer_v, final_norm_g):
    batch, seq, d_model = x.shape
    depth = attn_norm_g.shape[0]
    h = x.reshape(batch * seq, d_model)
    for l in range(depth):
        h_t = _mixer(h, seq, attn_norm_g[l], w_in[l], conv_w[l], sgu_w[l], sgu_b[l], sgu_norm_g[l],
                     out_norm_conv_g[l], out_norm_sgu_g[l], w_out[l])
        h = _peer(h_t, ffn_norm_g[l], peer_w_q[l], peer_sub_keys[l], peer_u[l], peer_v[l],
                  final_norm_g if l == depth - 1 else None)
    return h.reshape(batch, seq, d_model)
```

```python
import functools
import math

import jax
import jax.numpy as jnp
from jax import lax
from jax.experimental import pallas as pl
from jax.experimental.pallas import tpu as pltpu

F32 = jnp.float32
BF16 = jnp.bfloat16

EPS = 1e-6
CHUNK = 64
CONV_K = 3
CONV_GROUPS = 8
SGU_GROUPS = 4
SGU_BLOCK = 128
PEER_HEADS = 8
N_KEYS = 128
PEER_TOPK = 16
SQRT_HALF = math.sqrt(0.5)
NEG_BIG = -3.0e38
POS_BIG = 3.0e38

LANES = 128
SUBLANES = 8

MIXER_ROWS = 512
PEER_TOKENS = 512
PEER_KEY_ROWS = 8
MXU_PIECES = 4
GATE_ROWS = 4
GATE_KEYS = 32
MIXER_VMEM_BYTES = 48 * 1024 * 1024
PEER_VMEM_BYTES = 58 * 1024 * 1024

def _oddeven_merge(lo, hi, r):
    step = r * 2
    if step < hi - lo:
        yield from _oddeven_merge(lo, hi, step)
        yield from _oddeven_merge(lo + r, hi, step)
        yield from [(i, i + r) for i in range(lo + r, hi - r, step)]
    else:
        yield (lo, lo + r)


def _oddeven_merge_sort(lo, hi):
    if hi - lo >= 1:
        mid = lo + (hi - lo) // 2
        yield from _oddeven_merge_sort(lo, mid)
        yield from _oddeven_merge_sort(mid + 1, hi)
        yield from _oddeven_merge(lo, hi, 1)


_SORT_NET = tuple(_oddeven_merge_sort(0, PEER_TOPK - 1))
_BITONIC_NET = tuple((i, i | d) for d in (PEER_TOPK >> k for k in range(1, PEER_TOPK.bit_length()))
                     for i in range(PEER_TOPK) if i & d == 0)
_PAIR_ROWS = tuple(PEER_TOPK // (b + 1) for b in range(PEER_TOPK))


def _apply_net(net, xs):
    xs = list(xs)
    for i, j in net:
        a, b = xs[i], xs[j]
        xs[i], xs[j] = jnp.maximum(a, b), jnp.minimum(a, b)
    return xs


def _merge_top(a, b):
    return _apply_net(_BITONIC_NET, [jnp.maximum(a[j], b[PEER_TOPK - 1 - j]) for j in range(PEER_TOPK)])


def _top16_and_next(xs):
    groups = [_apply_net(_SORT_NET, xs[g:g + PEER_TOPK]) for g in range(0, len(xs), PEER_TOPK)]
    while len(groups) > 1:
        groups = [_merge_top(groups[g], groups[g + 1]) for g in range(0, len(groups), 2)]
    top = groups[0]
    nxt = jnp.full(top[0].shape, NEG_BIG, F32)
    for x in xs:
        nxt = jnp.maximum(nxt, jnp.where(x < top[-1], x, NEG_BIG))
    return top, nxt


def _gelu(a):
    return 0.5 * a * (1.0 + lax.erf(a * SQRT_HALF))


def _group_mean_sq(y, ones_ref, group):
    tot = jnp.dot((y * y).astype(BF16), ones_ref[...], preferred_element_type=F32)
    return tot * (1.0 / group)


def _mixer_kernel(x_ref, g_attn_ref, w_in_ref, conv_w_ref, sgu_w_ref, sgu_bias_ref, g_v_ref,
                  g_oc_ref, g_os_ref, ones_c_ref, ones_s_ref, w_out_ref, h_t_ref,
                  tail_ref, mixed_ref, *, rows, seq_blocks, conv_width, sgu_width):
    i = pl.program_id(0)

    @pl.when(i % seq_blocks == 0)
    def _():
        tail_ref[...] = jnp.zeros_like(tail_ref)

    x = x_ref[...]
    xn = x * lax.rsqrt(jnp.mean(x * x, axis=-1, keepdims=True) + EPS) * g_attn_ref[...]
    proj = jnp.dot(xn.astype(BF16), w_in_ref[...], preferred_element_type=F32)

    c0, c1, c2, c3 = conv_width, 2 * conv_width, 3 * conv_width, 3 * conv_width + sgu_width
    b_gate, c_gate, hc = proj[:, :c0], proj[:, c0:c1], proj[:, c1:c2]

    z = c_gate * hc
    tail = tail_ref[...]
    row = lax.broadcasted_iota(jnp.int32, z.shape, 0)
    z_m1 = jnp.where(row == 0, tail[SUBLANES - 1:SUBLANES, :], pltpu.roll(z, 1, 0))
    z_m2 = jnp.where(row == 0, tail[SUBLANES - 2:SUBLANES - 1, :],
                     jnp.where(row == 1, tail[SUBLANES - 1:SUBLANES, :], pltpu.roll(z, 2, 0)))
    cw = conv_w_ref[...]
    conv = cw[0:1, :] * z_m2 + cw[1:2, :] * z_m1 + cw[2:3, :] * z
    y_conv = b_gate * conv
    tail_ref[...] = z[rows - SUBLANES:, :]

    zz = _gelu(proj[:, c2:])
    u, v = zz[:, :sgu_width], zz[:, sgu_width:]
    gw = sgu_width // SGU_GROUPS
    vn = v * lax.rsqrt(_group_mean_sq(v, ones_s_ref, gw) + EPS) * g_v_ref[...]
    vnb = vn.astype(BF16)
    ri = lax.broadcasted_iota(jnp.int32, (SGU_BLOCK, SGU_BLOCK), 0) // CHUNK
    ci = lax.broadcasted_iota(jnp.int32, (SGU_BLOCK, SGU_BLOCK), 1) // CHUNK
    causal = ri >= ci
    n_pos = rows // SGU_BLOCK
    for g in range(SGU_GROUPS):
        wg = jnp.where(causal, sgu_w_ref[g], 0.0).astype(BF16)
        cols = slice(g * gw, (g + 1) * gw)
        stacked = jnp.concatenate(
            [vnb[pb * SGU_BLOCK:(pb + 1) * SGU_BLOCK, cols] for pb in range(n_pos)], axis=1)
        res = jnp.dot(wg, stacked, preferred_element_type=F32)
        for pb in range(n_pos):
            mixed_ref[pb * SGU_BLOCK:(pb + 1) * SGU_BLOCK, cols] = (
                res[:, pb * gw:(pb + 1) * gw] + sgu_bias_ref[:, cols])
    y_sgu = u * mixed_ref[...]

    yc = y_conv * lax.rsqrt(_group_mean_sq(y_conv, ones_c_ref, conv_width // CONV_GROUPS) + EPS) * g_oc_ref[...]
    ys = y_sgu * lax.rsqrt(_group_mean_sq(y_sgu, ones_s_ref, gw) + EPS) * g_os_ref[...]
    y = jnp.concatenate([yc, ys], axis=1).astype(BF16)
    h_t_ref[...] = (x + jnp.dot(y, w_out_ref[...], preferred_element_type=F32)).T


def _block_ones(width, group):
    idx = jnp.arange(width) // group
    return (idx[:, None] == idx[None, :]).astype(BF16)


def _mixer(h, seq, g_attn, w_in, conv_w, sgu_w, sgu_b, g_v, g_oc, g_os, w_out):
    n_tok, d_model = h.shape
    conv_width = conv_w.shape[1]
    sgu_width = g_v.shape[0]
    d_in = w_in.shape[1]
    rows = MIXER_ROWS
    assert seq % rows == 0 and rows % SGU_BLOCK == 0 and d_in == 3 * conv_width + 2 * sgu_width
    gw = sgu_width // SGU_GROUPS
    assert gw == SGU_BLOCK
    sgu_bias = jnp.repeat(sgu_b.T, gw, axis=1)
    const = lambda *shape: pl.BlockSpec(shape, lambda i: (0,) * len(shape))
    return pl.pallas_call(
        functools.partial(_mixer_kernel, rows=rows, seq_blocks=seq // rows,
                          conv_width=conv_width, sgu_width=sgu_width),
        grid=(n_tok // rows,),
        in_specs=[
            pl.BlockSpec((rows, d_model), lambda i: (i, 0)),
            const(1, d_model),
            const(d_model, d_in),
            const(CONV_K, conv_width),
            const(SGU_GROUPS, SGU_BLOCK, SGU_BLOCK),
            const(SGU_BLOCK, sgu_width),
            const(1, sgu_width),
            const(1, conv_width),
            const(1, sgu_width),
            const(conv_width, conv_width),
            const(sgu_width, sgu_width),
            const(conv_width + sgu_width, d_model),
        ],
        out_specs=pl.BlockSpec((d_model, rows), lambda i: (0, i)),
        out_shape=jax.ShapeDtypeStruct((d_model, n_tok), F32),
        scratch_shapes=[pltpu.VMEM((SUBLANES, conv_width), F32),
                        pltpu.VMEM((rows, sgu_width), F32)],
        compiler_params=pltpu.CompilerParams(dimension_semantics=("arbitrary",),
                                             vmem_limit_bytes=MIXER_VMEM_BYTES),
        name="mixer",
    )(h, g_attn.reshape(1, -1), w_in.astype(BF16), conv_w, sgu_w, sgu_bias, g_v.reshape(1, -1),
      g_oc.reshape(1, -1), g_os.reshape(1, -1),
      _block_ones(conv_width, conv_width // CONV_GROUPS), _block_ones(sgu_width, gw),
      w_out.astype(BF16))


def _peer_kernel(h_ref, g_ffn_ref, wq_t_ref, keys_ref, u_ref, v_t_ref, g_fin_ref, out_ref,
                 h_t_ref, hn_t_ref, acc_ref, s2_ref, e2_ref, theta_ref, e1_ref,
                 act0_ref, act1_ref, w0_ref, w1_ref, q_ref, sb0_ref, sb1_ref, top_ref, misc_ref,
                 u_buf, vt_buf, u_sem, vt_sem, *, tokens, n_chunks, final_norm):
    s = pl.program_id(1)
    lane_tiles = tokens // LANES
    act_refs = (act0_ref, act1_ref)
    w_refs = (w0_ref, w1_ref)
    chunk_rows = act0_ref.shape[0]
    d_model = acc_ref.shape[0]
    quarter_rows = chunk_rows // MXU_PIECES
    quarter_d = d_model // MXU_PIECES

    blk = pl.program_id(0)
    n_slots = u_buf.shape[0]

    def u_copy(c):
        return pltpu.make_async_copy(u_ref.at[pl.ds(c * chunk_rows, chunk_rows), :],
                                     u_buf.at[c % n_slots], u_sem.at[c % n_slots])

    def vt_copy(c):
        width = chunk_rows + LANES
        return pltpu.make_async_copy(v_t_ref.at[:, pl.ds(c * width, width)],
                                     vt_buf.at[c % n_slots], vt_sem.at[c % n_slots])

    @pl.when(jnp.logical_and(blk == 0, s == 0))
    def _():
        u_copy(0).start()
        u_copy(1).start()

    @pl.when(s < n_chunks)
    def _():
        u_copy(s).wait()
        vt_copy(s).start()

    @pl.when(s >= 2)
    def _():
        vt_copy(s - 2).wait()

    @pl.when(s + 2 < n_chunks)
    def _():
        u_copy(s + 2).start()

    @pl.when(jnp.logical_and(s >= n_chunks, blk < pl.num_programs(0) - 1))
    def _():
        u_copy(s - n_chunks).start()

    def lane_bcast(g_ref):
        g = g_ref[...]
        return jnp.concatenate([g] * lane_tiles, axis=1)

    def route():
        h_t = h_ref[...]
        h_t_ref[...] = h_t
        ms = jnp.mean(h_t * h_t, axis=0, keepdims=True)
        hn_t_ref[:, 0:tokens] = (h_t * lax.rsqrt(ms + EPS) * lane_bcast(g_ffn_ref)).astype(BF16)
        acc_ref[:, 0:tokens] = jnp.zeros(h_t.shape, F32)

        n_q = PEER_HEADS * N_KEYS
        for p in (1, 0):
            q = jnp.dot(wq_t_ref[p * n_q:(p + 1) * n_q, :], hn_t_ref[:, 0:tokens], preferred_element_type=F32)
            q_ref[p] = q.astype(BF16)
        score_refs = (sb0_ref, sb1_ref)
        for p in (1, 0):
            for hd in range(PEER_HEADS):
                sc = jnp.dot(keys_ref[p * PEER_HEADS + hd], q_ref[p, hd * N_KEYS:(hd + 1) * N_KEYS, :],
                             preferred_element_type=F32)
                if p == 1:
                    s2_ref[hd, :, 0:tokens] = sc
                for lt in range(lane_tiles):
                    score_refs[p][lt, pl.ds(hd, N_KEYS, stride=PEER_HEADS), :] = sc[:, lt * LANES:(lt + 1) * LANES]

        def lanes_of(lt):
            return pl.ds(pl.multiple_of(lt * LANES, LANES), LANES)

        def sort_half(p):
            def body(lt, carry):
                xs = [score_refs[p][lt, k * PEER_HEADS:(k + 1) * PEER_HEADS, :] for k in range(N_KEYS)]
                top, nxt = _top16_and_next(xs)
                ls = lanes_of(lt)
                for j, val in enumerate(top + [nxt]):
                    top_ref[p, j * PEER_HEADS:(j + 1) * PEER_HEADS, ls] = val
                return carry
            return body

        def thresholds(lt, carry):
            act_piece(0, lt)
            ls = lanes_of(lt)
            tops, mids = [], []
            for p in range(2):
                top = [top_ref[p, j * PEER_HEADS:(j + 1) * PEER_HEADS, ls] for j in range(PEER_TOPK + 1)]
                mids.append([0.5 * (top[j] + top[j + 1]) for j in range(PEER_TOPK)])
                tops.append(top[:PEER_TOPK])
            v1, v2 = tops
            pair = {(a, b): v1[a] + v2[b] for b in range(PEER_TOPK) for a in range(_PAIR_ROWS[b])}
            neg = jnp.full(v1[0].shape, NEG_BIG, F32)
            best = [pair[(a, 0)] for a in range(PEER_TOPK)]
            rest = [pair[(a, b)] for b in range(1, PEER_TOPK) for a in range(_PAIR_ROWS[b])]
            rest = rest + [neg] * (-len(rest) % PEER_TOPK)
            for g in range(0, len(rest), PEER_TOPK):
                best = _merge_top(best, _apply_net(_SORT_NET, rest[g:g + PEER_TOPK]))
            t16 = best[-1]
            z = jnp.exp(best[0] - best[0])
            for k in range(1, PEER_TOPK):
                z = z + jnp.exp(best[k] - best[0])
            thr = []
            for a in range(PEER_TOPK):
                cnt = jnp.zeros(t16.shape, F32)
                for b in range(PEER_TOPK // (a + 1)):
                    cnt = cnt + jnp.where(pair[(a, b)] >= t16, 1.0, 0.0)
                th = jnp.full(t16.shape, POS_BIG, F32)
                for j in range(PEER_TOPK // (a + 1)):
                    th = jnp.where(cnt >= float(j + 1), mids[1][j], th)
                thr.append(th)
            misc_ref[0:PEER_HEADS, ls] = v2[0]
            misc_ref[PEER_HEADS:2 * PEER_HEADS, ls] = 1.0 / z
            for k in range(N_KEYS):
                rows = slice(k * PEER_HEADS, (k + 1) * PEER_HEADS)
                x = sb0_ref[lt, rows, :]
                theta = jnp.full(x.shape, POS_BIG, F32)
                for a in reversed(range(PEER_TOPK)):
                    theta = jnp.where(x >= mids[0][a], thr[a], theta)
                theta_ref[lt, rows, :] = theta
                e1_ref[lt, rows, :] = SQRT_HALF * jnp.exp(x - v1[0])
            return carry

        lax.fori_loop(0, lane_tiles, sort_half(1), 0)
        lax.fori_loop(0, lane_tiles, sort_half(0), 0)
        lax.fori_loop(0, lane_tiles, thresholds, 0)

        for hd in range(PEER_HEADS):
            m2 = misc_ref[hd:hd + 1, :]
            z_inv = misc_ref[PEER_HEADS + hd:PEER_HEADS + hd + 1, :]
            e2_ref[hd, :, 0:tokens] = jnp.exp(s2_ref[hd, :, 0:tokens] - m2) * z_inv

    def act_piece(slot, mq):
        rows = pl.ds(pl.multiple_of(mq * quarter_rows, quarter_rows), quarter_rows)
        act_refs[slot][rows, 0:tokens] = jnp.dot(u_buf[s % n_slots, rows, 0:d_model], hn_t_ref[:, 0:tokens],
                                                 preferred_element_type=F32)

    def out_piece(slot, mq):
        rows = slice(mq * quarter_d, (mq + 1) * quarter_d)
        acc_ref[rows, 0:tokens] += jnp.dot(vt_buf[(s - 2) % n_slots, rows, 0:chunk_rows], w_refs[slot][:, 0:tokens],
                                           preferred_element_type=F32)

    def w_block(key_base, slot, r0, lt, q0):
        ls = slice(lt * LANES, (lt + 1) * LANES)
        qs = slice(q0, q0 + GATE_KEYS)
        gates = [jnp.zeros((GATE_KEYS, LANES), F32) for _ in range(GATE_ROWS)]
        for hd in range(PEER_HEADS):
            s2 = s2_ref[hd, qs, ls]
            e2 = e2_ref[hd, qs, ls]
            for k in range(GATE_ROWS):
                row = pl.ds(key_base + (r0 + k) * PEER_HEADS + hd, 1)
                theta = jnp.broadcast_to(theta_ref[lt, row, :], (GATE_KEYS, LANES))
                e1 = jnp.broadcast_to(e1_ref[lt, row, :], (GATE_KEYS, LANES))
                gates[k] = gates[k] + jnp.where(s2 >= theta, e2, 0.0) * e1
        for k in range(GATE_ROWS):
            rows = slice((r0 + k) * N_KEYS + q0, (r0 + k) * N_KEYS + q0 + GATE_KEYS)
            x = act_refs[slot][rows, ls]
            w_refs[slot][rows, ls] = (gates[k] * (x * (1.0 + lax.erf(x)))).astype(BF16)

    def step(act_slot=None, build=None, out_slot=None):
        mxu = []
        for mq in range(MXU_PIECES):
            if act_slot is not None:
                mxu.append(functools.partial(act_piece, act_slot, mq))
            if out_slot is not None:
                mxu.append(functools.partial(out_piece, out_slot, mq))
        tiles = []
        if build is not None:
            chunk, slot = build
            key_base = pl.multiple_of(chunk * (PEER_KEY_ROWS * PEER_HEADS), PEER_KEY_ROWS * PEER_HEADS)
            tiles = [functools.partial(w_block, key_base, slot, r0, lt, q0)
                     for lt in range(lane_tiles)
                     for r0 in range(0, PEER_KEY_ROWS, GATE_ROWS)
                     for q0 in range(0, N_KEYS, GATE_KEYS)]
        n_groups = max(len(mxu), 1)
        per_group = -(-len(tiles) // n_groups)
        lead = per_group // 2 if mxu else 0
        for tile in tiles[:lead]:
            tile()
        for g in range(n_groups):
            if g < len(mxu):
                mxu[g]()
            for tile in tiles[lead + g * per_group:lead + (g + 1) * per_group]:
                tile()

    @pl.when(s == 0)
    def _():
        route()

    @pl.when(s == 1)
    def _():
        step(act_slot=1, build=(0, 0))

    for parity in (0, 1):
        @pl.when(jnp.logical_and(jnp.logical_and(s >= 2, s < n_chunks), s % 2 == parity))
        def _():
            step(act_slot=parity, build=(s - 1, 1 - parity), out_slot=parity)

    @pl.when(s == n_chunks)
    def _():
        step(build=(n_chunks - 1, (n_chunks - 1) % 2), out_slot=n_chunks % 2)

    @pl.when(s == n_chunks + 1)
    def _():
        step(out_slot=(n_chunks + 1) % 2)
        res = h_t_ref[...] + acc_ref[:, 0:tokens]
        if final_norm:
            ms = jnp.mean(res * res, axis=0, keepdims=True)
            res = res * lax.rsqrt(ms + EPS) * lane_bcast(g_fin_ref)
        out_ref[...] = res.T


def _expert_operands_kernel(u_ref, v_ref, u_out_ref, v_t_out_ref):
    chunk, d_model = u_ref.shape
    u_out_ref[:, 0:d_model] = (SQRT_HALF * u_ref[...]).astype(BF16)
    u_out_ref[:, d_model:] = jnp.zeros((chunk, LANES), BF16)
    v_t_out_ref[:, 0:chunk] = v_ref[...].T.astype(BF16)
    v_t_out_ref[:, chunk:] = jnp.zeros((d_model, LANES), BF16)


def _expert_operands(expert_u, expert_v, chunk):
    n_experts, d_model = expert_u.shape
    n_chunks = n_experts // chunk
    return pl.pallas_call(
        _expert_operands_kernel,
        grid=(n_chunks,),
        in_specs=[pl.BlockSpec((chunk, d_model), lambda c: (c, 0)),
                  pl.BlockSpec((chunk, d_model), lambda c: (c, 0))],
        out_specs=[pl.BlockSpec((chunk, d_model + LANES), lambda c: (c, 0)),
                   pl.BlockSpec((d_model, chunk + LANES), lambda c: (0, c))],
        out_shape=[jax.ShapeDtypeStruct((n_experts, d_model + LANES), BF16),
                   jax.ShapeDtypeStruct((d_model, n_chunks * (chunk + LANES)), BF16)],
        compiler_params=pltpu.CompilerParams(dimension_semantics=("arbitrary",),
                                             vmem_limit_bytes=MIXER_VMEM_BYTES),
        name="expert_operands",
    )(expert_u, expert_v)


def _query_operand_kernel(w_ref, w_t_ref):
    w_t_ref[...] = w_ref[...].T.astype(BF16)


def _query_operand(w_q, d_half):
    d_model, n_cols = w_q.shape
    n_blocks = n_cols // d_half
    return pl.pallas_call(
        _query_operand_kernel,
        grid=(n_blocks,),
        in_specs=[pl.BlockSpec((d_model, d_half), lambda c: (0, c))],
        out_specs=pl.BlockSpec((d_half, d_model), lambda c: ((c % 2) * (n_blocks // 2) + c // 2, 0)),
        out_shape=jax.ShapeDtypeStruct((n_cols, d_model), BF16),
        compiler_params=pltpu.CompilerParams(dimension_semantics=("arbitrary",)),
        name="query_operand",
    )(w_q)


def _peer(h_t, g_ffn, w_q, sub_keys, expert_u, expert_v, g_final):
    d_model, n_tok = h_t.shape
    n_experts = expert_u.shape[0]
    d_half = sub_keys.shape[-1]
    assert sub_keys.shape[:3] == (PEER_HEADS, 2, N_KEYS) and d_half == N_KEYS
    assert n_experts == N_KEYS * N_KEYS and w_q.shape == (d_model, PEER_HEADS * 2 * d_half)
    tokens = PEER_TOKENS
    chunk = PEER_KEY_ROWS * N_KEYS
    n_chunks = n_experts // chunk
    assert n_tok % tokens == 0 and n_chunks >= 3 and tokens // LANES == MXU_PIECES
    final_norm = g_final is not None
    wq_t = _query_operand(w_q, d_half)
    keys = sub_keys.transpose(1, 0, 2, 3).reshape(2 * PEER_HEADS, N_KEYS, d_half).astype(BF16)
    g_fin = g_final if final_norm else jnp.ones((d_model,), F32)
    lanes = lambda g: jnp.broadcast_to(g.astype(F32)[:, None], (d_model, LANES))
    const = lambda *shape, **kw: pl.BlockSpec(shape, lambda i, s: (0,) * len(shape), **kw)
    padded = tokens + LANES
    u_pad, v_t_pad = _expert_operands(expert_u, expert_v, chunk)
    table = lambda dtype: pltpu.VMEM((PEER_HEADS, N_KEYS, padded), dtype)
    return pl.pallas_call(
        functools.partial(_peer_kernel, tokens=tokens, n_chunks=n_chunks, final_norm=final_norm),
        grid=(n_tok // tokens, n_chunks + 2),
        in_specs=[
            pl.BlockSpec((d_model, tokens), lambda i, s: (0, i), pipeline_mode=pl.Buffered(1)),
            const(d_model, LANES),
            const(PEER_HEADS * 2 * d_half, d_model, pipeline_mode=pl.Buffered(1)),
            const(2 * PEER_HEADS, N_KEYS, d_half),
            pl.BlockSpec(memory_space=pl.ANY),
            pl.BlockSpec(memory_space=pl.ANY),
            const(d_model, LANES),
        ],
        out_specs=pl.BlockSpec((tokens, d_model), lambda i, s: (i, 0)),
        out_shape=jax.ShapeDtypeStruct((n_tok, d_model), F32),
        scratch_shapes=[
            pltpu.VMEM((d_model, tokens), F32),
            pltpu.VMEM((d_model, padded), BF16),
            pltpu.VMEM((d_model, padded), F32),
            table(F32),
            table(F32),
            pltpu.VMEM((tokens // LANES, N_KEYS * PEER_HEADS, LANES), F32),
            pltpu.VMEM((tokens // LANES, N_KEYS * PEER_HEADS, LANES), F32),
            pltpu.VMEM((chunk, padded), F32),
            pltpu.VMEM((chunk, padded), F32),
            pltpu.VMEM((chunk, padded), BF16),
            pltpu.VMEM((chunk, padded), BF16),
            pltpu.VMEM((2, PEER_HEADS * N_KEYS, tokens), BF16),
            pltpu.VMEM((tokens // LANES, PEER_HEADS * N_KEYS, LANES), F32),
            pltpu.VMEM((tokens // LANES, PEER_HEADS * N_KEYS, LANES), F32),
            pltpu.VMEM((2, (PEER_TOPK + 1) * PEER_HEADS, tokens), F32),
            pltpu.VMEM((2 * PEER_HEADS, tokens), F32),
            pltpu.VMEM((3, chunk, d_model + LANES), BF16),
            pltpu.VMEM((3, d_model, chunk + LANES), BF16),
            pltpu.SemaphoreType.DMA((3,)),
            pltpu.SemaphoreType.DMA((3,)),
        ],
        compiler_params=pltpu.CompilerParams(dimension_semantics=("arbitrary", "arbitrary"),
                                             vmem_limit_bytes=PEER_VMEM_BYTES),
        name="peer",
    )(h_t, lanes(g_ffn), wq_t, keys,
      u_pad, v_t_pad, lanes(g_fin))


def kernel(x, attn_norm_g, w_in, conv_w, sgu_w, sgu_b, sgu_norm_g, out_norm_conv_g, out_norm_sgu_g,
           w_out, ffn_norm_g, peer_w_q, peer_sub_keys, peer_u, peer_v, final_norm_g):
    batch, seq, d_model = x.shape
    depth = attn_norm_g.shape[0]
    h = x.reshape(batch * seq, d_model)
    for l in range(depth):
        h_t = _mixer(h, seq, attn_norm_g[l], w_in[l], conv_w[l], sgu_w[l], sgu_b[l], sgu_norm_g[l],
                     out_norm_conv_g[l], out_norm_sgu_g[l], w_out[l])
        h = _peer(h_t, ffn_norm_g[l], peer_w_q[l], peer_sub_keys[l], peer_u[l], peer_v[l],
                  final_norm_g if l == depth - 1 else None)
    return h.reshape(batch, seq, d_model)
```
